```python
import jax, jax.numpy as jnp
from jax import lax
import numpy as np

D_MODEL = 1024
BATCH = 16
SEQ = 4096
DEPTH = 4

CHUNK = 64
Q_BLOCK = 128
ROPE_THETA = 10000.0
EPS = 1e-6

A_HEADS = 6
A_HEAD_DIM = 64
A_WIDTH = A_HEADS * A_HEAD_DIM
IDX_HEADS = 8
IDX_DIM = 32
TOPK_MAX = 256

S5_GROUP = 16
S5_WIDTH = 256
S5_GROUPS = S5_WIDTH // S5_GROUP
S5_STATE = 64
STEP_MIN = 1e-3
STEP_MAX = 1e-1

C_HEADS = 6
C_NOPE = 64
C_ROPE = 32
C_VDIM = 64
C_QK = C_NOPE + C_ROPE
C_WIDTH = C_HEADS * C_VDIM
Q_LORA = 256
KV_LORA = 128

D_MIX = A_WIDTH + S5_WIDTH + C_WIDTH

SPLIT_SIZES = (
    A_WIDTH,
    A_HEAD_DIM,
    A_HEAD_DIM,
    IDX_HEADS * IDX_DIM,
    IDX_DIM,
    IDX_HEADS,
    A_WIDTH,
    S5_WIDTH,
    S5_WIDTH,
    Q_LORA,
    KV_LORA,
    C_ROPE,
    C_WIDTH,
)
D_IN = sum(SPLIT_SIZES)

kernel_name = "hybrid_dsa_s5_mla_parallel_heads"


def _split_points():
    return [int(v) for v in np.cumsum(SPLIT_SIZES)[:-1]]


def rms_norm(x, g):
    xf = x.astype(jnp.float32)
    y = xf * lax.rsqrt(jnp.mean(xf * xf, axis=-1, keepdims=True) + EPS)
    return (y * g.astype(jnp.float32)).astype(x.dtype)


def apply_rope(x, pos):
    d = x.shape[-1]
    half = d // 2
    inv = ROPE_THETA ** (-jnp.arange(half, dtype=jnp.float32) * 2.0 / d)
    ang = pos.astype(jnp.float32)[:, None] * inv[None, :]
    cos = jnp.cos(ang)[:, None, :]
    sin = jnp.sin(ang)[:, None, :]
    xf = x.astype(jnp.float32)
    x1, x2 = xf[..., :half], xf[..., half:]
    return jnp.concatenate([x1 * cos - x2 * sin, x2 * cos + x1 * sin], axis=-1).astype(x.dtype)


def to_blocks(t):
    b, s = t.shape[:2]
    return t.reshape((b, s // Q_BLOCK, Q_BLOCK) + t.shape[2:]).swapaxes(0, 1)


def from_blocks(t):
    nb, b, q = t.shape[:3]
    return t.swapaxes(0, 1).reshape((b, nb * q) + t.shape[3:])


def gather_rows(src, idx):
    return jax.vmap(lambda s_, i_: s_[i_])(src, idx)


def dsa_branch(qa, ka, va, iq, ik, iw, q_norm_g, k_norm_g, pos, limit):
    b, s, _ = qa.shape
    q = apply_rope(rms_norm(qa.reshape(b, s, A_HEADS, A_HEAD_DIM), q_norm_g), pos)
    k = apply_rope(rms_norm(ka.reshape(b, s, 1, A_HEAD_DIM), k_norm_g), pos)[:, :, 0]
    v = va
    iq = apply_rope(iq.reshape(b, s, IDX_HEADS, IDX_DIM), pos)
    ik = apply_rope(ik[:, :, None, :], pos)[:, :, 0].astype(jnp.float32)
    iw = iw.astype(jnp.float32) * (IDX_HEADS ** -0.5)
    k_top = min(TOPK_MAX, s // 4)
    key_pos = jnp.arange(s)

    def one_block(args):
        qb, iqb, iwb, limb = args
        sc = jnp.einsum('bthd,bsd->bths', iqb.astype(jnp.float32), ik) * (IDX_DIM ** -0.5)
        sc = jnp.einsum('bths,bth->bts', jax.nn.relu(sc), iwb)
        admissible = key_pos[None, :] < limb[:, None]
        sc = jnp.where(admissible[None], sc, -jnp.inf)
        _, idx = lax.top_k(sc, k_top)
        valid = idx < limb[None, :, None]
        k_sel = gather_rows(k, idx)
        v_sel = gather_rows(v, idx)
        att = jnp.einsum('bthd,btkd->bthk', qb, k_sel).astype(jnp.float32) * (A_HEAD_DIM ** -0.5)
        att = jnp.where(valid[:, :, None, :], att, -jnp.inf)
        p = jax.nn.softmax(att, axis=-1).astype(v_sel.dtype)
        return jnp.einsum('bthk,btkd->bthd', p, v_sel)

    out = lax.map(one_block, (to_blocks(q), to_blocks(iq), to_blocks(iw),
                              limit.reshape(s // Q_BLOCK, Q_BLOCK)))
    return from_blocks(out).reshape(b, s, A_WIDTH)


def s5_branch(u, a_re, a_im, b_re, b_im, c_re, c_im, d_skip, log_step, w_glu):
    b, s, _ = u.shape
    ug = u.reshape(b, s, S5_GROUPS, S5_GROUP).astype(jnp.float32)
    ar = a_re.astype(jnp.float32)
    ai = a_im.astype(jnp.float32)
    step = jnp.exp(log_step.astype(jnp.float32))[:, None]
    mag = jnp.exp(ar * step)
    abar_re = mag * jnp.cos(ai * step)
    abar_im = mag * jnp.sin(ai * step)
    den = ar * ar + ai * ai
    nr = abar_re - 1.0
    f_re = (nr * ar + abar_im * ai) / den
    f_im = (abar_im * ar - nr * ai) / den
    bu_re = jnp.einsum('bsgc,gpc->bsgp', ug, b_re.astype(jnp.float32))
    bu_im = jnp.einsum('bsgc,gpc->bsgp', ug, b_im.astype(jnp.float32))
    bb_re = f_re * bu_re - f_im * bu_im
    bb_im = f_re * bu_im + f_im * bu_re
    aa_re = jnp.broadcast_to(abar_re, bb_re.shape)
    aa_im = jnp.broadcast_to(abar_im, bb_im.shape)

    def combine(e1, e2):
        a1r, a1i, b1r, b1i = e1
        a2r, a2i, b2r, b2i = e2
        return (a2r * a1r - a2i * a1i,
                a2r * a1i + a2i * a1r,
                a2r * b1r - a2i * b1i + b2r,
                a2r * b1i + a2i * b1r + b2i)

    _, _, xr, xi = lax.associative_scan(combine, (aa_re, aa_im, bb_re, bb_im), axis=1)
    y = (jnp.einsum('bsgp,gcp->bsgc', xr, c_re.astype(jnp.float32))
         - jnp.einsum('bsgp,gcp->bsgc', xi, c_im.astype(jnp.float32))
         + d_skip.astype(jnp.float32) * ug)
    y = y.reshape(b, s, S5_WIDTH)
    g = jax.nn.gelu(y)
    y = g * jax.nn.sigmoid(g @ w_glu.astype(jnp.float32))
    return y.astype(u.dtype)


def mla_branch(cq, ckv, kpe, q_lora_g, kv_lora_g, w_uq, w_ukv, q_norm_g, k_norm_g, pos, limit):
    b, s, _ = cq.shape
    q = (rms_norm(cq, q_lora_g) @ w_uq).reshape(b, s, C_HEADS, C_QK)
    kv = (rms_norm(ckv, kv_lora_g) @ w_ukv).reshape(b, s, C_HEADS, C_NOPE + C_VDIM)
    k_nope, v = kv[..., :C_NOPE], kv[..., C_NOPE:]
    k_rope = jnp.broadcast_to(kpe[:, :, None, :], (b, s, C_HEADS, C_ROPE))
    k = jnp.concatenate([k_nope, k_rope], axis=-1)
    q = rms_norm(q, q_norm_g)
    k = rms_norm(k, k_norm_g)
    q = jnp.concatenate([q[..., :C_NOPE], apply_rope(q[..., C_NOPE:], pos)], axis=-1)
    k = jnp.concatenate([k[..., :C_NOPE], apply_rope(k[..., C_NOPE:], pos)], axis=-1)
    key_pos = jnp.arange(s)

    def one_block(args):
        qb, limb = args
        att = jnp.einsum('bthd,bshd->bhts', qb, k).astype(jnp.float32) * (C_QK ** -0.5)
        mask = key_pos[None, :] < limb[:, None]
        att = jnp.where(mask[None, None], att, -jnp.inf)
        p = jax.nn.softmax(att, axis=-1).astype(v.dtype)
        return jnp.einsum('bhts,bshd->bthd', p, v)

    out = lax.map(one_block, (to_blocks(q), limit.reshape(s // Q_BLOCK, Q_BLOCK)))
    return from_blocks(out).reshape(b, s, C_WIDTH)


def setup_inputs(seed: int = 0) -> dict:
    key = jax.random.key(seed)
    ks = jax.random.split(key, 24)
    f32 = jnp.float32
    nrm = lambda k, shape, sc: jax.random.normal(k, shape, f32) * sc
    gain = lambda k, shape: 1.0 + 0.05 * jax.random.normal(k, shape, f32)
    L, G, P, Cg = DEPTH, S5_GROUPS, S5_STATE, S5_GROUP
    a_re = -0.5 + 0.01 * jax.random.normal(ks[11], (L, G, P), f32)
    a_im = jnp.pi * jnp.arange(P, dtype=f32)[None, None, :] + 0.01 * jax.random.normal(ks[12], (L, G, P), f32)
    log_step = jax.random.uniform(ks[19], (L, G), f32, np.log(STEP_MIN), np.log(STEP_MAX))
    return {
        "x": jax.random.normal(ks[0], (BATCH, SEQ, D_MODEL), f32),
        "norm_g": gain(ks[1], (L, D_MODEL)),
        "w_in": nrm(ks[2], (L, D_MODEL, D_IN), D_MODEL ** -0.5),
        "attn_q_norm": gain(ks[3], (L, A_HEAD_DIM)),
        "attn_k_norm": gain(ks[4], (L, A_HEAD_DIM)),
        "mla_q_lora_norm": gain(ks[5], (L, Q_LORA)),
        "mla_kv_lora_norm": gain(ks[6], (L, KV_LORA)),
        "mla_w_uq": nrm(ks[7], (L, Q_LORA, C_HEADS * C_QK), Q_LORA ** -0.5),
        "mla_w_ukv": nrm(ks[8], (L, KV_LORA, C_HEADS * (C_NOPE + C_VDIM)), KV_LORA ** -0.5),
        "mla_q_norm": gain(ks[9], (L, C_QK)),
        "mla_k_norm": gain(ks[10], (L, C_QK)),
        "ssm_a_re": a_re,
        "ssm_a_im": a_im,
        "ssm_b_re": nrm(ks[13], (L, G, P, Cg), (2.0 * Cg) ** -0.5),
        "ssm_b_im": nrm(ks[14], (L, G, P, Cg), (2.0 * Cg) ** -0.5),
        "ssm_c_re": nrm(ks[15], (L, G, Cg, P), P ** -0.5),
        "ssm_c_im": nrm(ks[16], (L, G, Cg, P), P ** -0.5),
        "ssm_d": nrm(ks[17], (L, G, Cg), 1.0),
        "ssm_log_step": log_step,
        "ssm_w_glu": nrm(ks[18], (L, S5_WIDTH, S5_WIDTH), S5_WIDTH ** -0.5),
        "w_out": nrm(ks[20], (L, D_MIX, D_MODEL), 0.5 * D_MIX ** -0.5),
    }


def reference(x, norm_g, w_in, attn_q_norm, attn_k_norm, mla_q_lora_norm, mla_kv_lora_norm,
              mla_w_uq, mla_w_ukv, mla_q_norm, mla_k_norm, ssm_a_re, ssm_a_im, ssm_b_re,
              ssm_b_im, ssm_c_re, ssm_c_im, ssm_d, ssm_log_step, ssm_w_glu, w_out):
    s = x.shape[1]
    pos = jnp.arange(s, dtype=jnp.int32)
    limit = (pos // CHUNK + 1) * CHUNK
    points = _split_points()
    for l in range(DEPTH):
        h = rms_norm(x, norm_g[l])
        proj = h @ w_in[l]
        (qa, ka, va, iq, ik, iw, ga, u, gb, cq, ckv, kpe, gc) = jnp.split(proj, points, axis=-1)
        ya = dsa_branch(qa, ka, va, iq, ik, iw, attn_q_norm[l], attn_k_norm[l], pos, limit)
        yb = s5_branch(u, ssm_a_re[l], ssm_a_im[l], ssm_b_re[l], ssm_b_im[l], ssm_c_re[l],
                       ssm_c_im[l], ssm_d[l], ssm_log_step[l], ssm_w_glu[l])
        yc = mla_branch(cq, ckv, kpe, mla_q_lora_norm[l], mla_kv_lora_norm[l], mla_w_uq[l],
                        mla_w_ukv[l], mla_q_norm[l], mla_k_norm[l], pos, limit)
        mixed = jnp.concatenate([ya * jax.nn.silu(ga), yb * jax.nn.silu(gb),
                                 yc * jax.nn.silu(gc)], axis=-1)
        x = x + mixed @ w_out[l]
    return x
```

```python
import functools

import numpy as np
import jax
import jax.numpy as jnp
from jax import lax
from jax.experimental import pallas as pl
from jax.experimental.pallas import tpu as pltpu

F32 = jnp.float32
BF16 = jnp.bfloat16
I32 = jnp.int32

CHUNK = 64
ROPE_THETA = 10000.0
EPS = 1e-6

A_HEADS = 6
A_HD = 64
A_WIDTH = A_HEADS * A_HD
IDX_HEADS = 8
IDX_DIM = 32
TOPK_MAX = 256

S5_GROUP = 16
S5_WIDTH = 256
S5_GROUPS = S5_WIDTH // S5_GROUP
S5_STATE = 64
S5_CH = S5_GROUPS * S5_STATE

C_HEADS = 6
C_NOPE = 64
C_ROPE = 32
C_VDIM = 64
C_QK = C_NOPE + C_ROPE
C_QK_PAD = 128
C_WIDTH = C_HEADS * C_VDIM
Q_LORA = 256
KV_LORA = 128

SPLIT_SIZES = (A_WIDTH, A_HD, A_HD, IDX_HEADS * IDX_DIM, IDX_DIM, IDX_HEADS, A_WIDTH,
               S5_WIDTH, S5_WIDTH, Q_LORA, KV_LORA, C_ROPE, C_WIDTH)

T_QA, T_KA, T_VA, T_IQ, T_IK, T_CQ, T_CKV, T_KPE, T_IW, T_END = (
    0, 384, 448, 512, 768, 800, 1056, 1184, 1216, 1232)
G_AQ, G_AK, G_QL, G_KVL, G_MQ, G_MK, G_END = 0, 64, 128, 384, 512, 608, 704
R_C64, R_S64, R_C32, R_S32, R_END = 0, 32, 64, 80, 96

INT_MIN = np.int32(-2 ** 31)
NEG_INF = float("-inf")

VMEM_LIMIT = 56 * 1024 * 1024


def _cparams(sem):
    return pltpu.CompilerParams(dimension_semantics=sem, vmem_limit_bytes=VMEM_LIMIT)


def _rms_rows(x, gcol):
    r = lax.rsqrt(jnp.mean(x * x, axis=0, keepdims=True) + EPS)
    return x * r * gcol


def _rope_rows(x, cos, sin):
    half = x.shape[0] // 2
    x1, x2 = x[:half], x[half:]
    return jnp.concatenate([x1 * cos - x2 * sin, x2 * cos + x1 * sin], axis=0)


def _rows_to_token_major(xt, width):
    d, t = xt.shape
    if d < 128:
        xt = jnp.concatenate([xt, jnp.zeros((128 - d, t), F32)], axis=0)
    return xt.T[:, :width]


def _prep_kernel(x_ref, ng_ref, wn_ref, wt_ref, gcol_ref, rope_ref, wuq_ref, wukv_ref,
                 g_out, u_out, qt_out, k_out, vt_out, iqt_out, ik_out, iwt_out,
                 cqt_out, ck_out, cvt_out):
    x = x_ref[0]
    ms = jnp.mean(x * x, axis=-1, keepdims=True)
    h = (x * lax.rsqrt(ms + EPS) * ng_ref[...]).astype(BF16)

    pn = jnp.dot(h, wn_ref[...], preferred_element_type=F32)
    g_out[0] = pn[:, :1024].astype(BF16)
    u_out[...] = pn[:, 1024:1280]

    def tdot(lo, hi):
        return lax.dot_general(wt_ref[lo:hi, :], h, (((1,), (1,)), ((), ())),
                               preferred_element_type=F32)

    cos64 = rope_ref[R_C64:R_S64, :]
    sin64 = rope_ref[R_S64:R_C32, :]
    cos32 = rope_ref[R_C32:R_S32, :]
    sin32 = rope_ref[R_S32:R_END, :]

    pa = tdot(T_QA, T_IQ)
    g_aq = gcol_ref[G_AQ:G_AK, :]
    for hd in range(A_HEADS):
        qh = _rope_rows(_rms_rows(pa[hd * A_HD:(hd + 1) * A_HD], g_aq), cos64, sin64)
        qt_out[0, hd * A_HD:(hd + 1) * A_HD, :] = (qh * (A_HD ** -0.5)).astype(BF16)
    kt = _rope_rows(_rms_rows(pa[T_KA:T_VA], gcol_ref[G_AK:G_QL, :]), cos64, sin64)
    k_out[0] = _rows_to_token_major(kt, A_HD).astype(BF16)
    vt_out[0] = pa[T_VA:T_IQ].astype(BF16)

    pb = tdot(T_IQ, T_CQ)
    for hd in range(IDX_HEADS):
        iqh = _rope_rows(pb[hd * IDX_DIM:(hd + 1) * IDX_DIM], cos32, sin32)
        iqt_out[0, hd * IDX_DIM:(hd + 1) * IDX_DIM, :] = iqh.astype(BF16)
    ikt = _rope_rows(pb[T_IK - T_IQ:T_CQ - T_IQ], cos32, sin32)
    ik_out[0] = _rows_to_token_major(ikt, IDX_DIM).astype(BF16)

    pc = tdot(T_CQ, T_END)
    iwt_out[0] = pc[T_IW - T_CQ:T_IW - T_CQ + IDX_HEADS] * ((IDX_HEADS ** -0.5) * (IDX_DIM ** -0.5))

    cqn = _rms_rows(pc[:Q_LORA], gcol_ref[G_QL:G_KVL, :]).astype(BF16)
    qt = jnp.dot(wuq_ref[...], cqn, preferred_element_type=F32)
    g_mq = gcol_ref[G_MQ:G_MK, :]
    zpad = jnp.zeros((C_QK_PAD - C_QK, x.shape[0]), F32)
    for hd in range(C_HEADS):
        qh = _rms_rows(qt[hd * C_QK_PAD:hd * C_QK_PAD + C_QK], g_mq)
        qh = jnp.concatenate([qh[:C_NOPE], _rope_rows(qh[C_NOPE:], cos32, sin32), zpad], axis=0)
        cqt_out[0, hd * C_QK_PAD:(hd + 1) * C_QK_PAD, :] = (qh * (C_QK ** -0.5)).astype(BF16)

    ckvn = _rms_rows(pc[T_CKV - T_CQ:T_KPE - T_CQ], gcol_ref[G_KVL:G_MQ, :]).astype(BF16)
    kvt = jnp.dot(wukv_ref[...], ckvn, preferred_element_type=F32)
    kpe = pc[T_KPE - T_CQ:T_IW - T_CQ]
    kpe_ss = jnp.sum(kpe * kpe, axis=0, keepdims=True)
    g_mk = gcol_ref[G_MK:G_END, :]
    for hd in range(C_HEADS):
        kn = kvt[hd * 128:hd * 128 + C_NOPE]
        ss = jnp.sum(kn * kn, axis=0, keepdims=True) + kpe_ss
        r = lax.rsqrt(ss * (1.0 / C_QK) + EPS)
        kh = jnp.concatenate([kn * r * g_mk[:C_NOPE],
                              _rope_rows(kpe * r * g_mk[C_NOPE:], cos32, sin32), zpad], axis=0)
        ck_out[0, hd] = kh.T.astype(BF16)
        cvt_out[0, hd * C_VDIM:(hd + 1) * C_VDIM, :] = kvt[hd * 128 + C_NOPE:(hd + 1) * 128].astype(BF16)


def _prep_call(x, ng, wn, wt, gcol, rope, wuq, wukv, ts):
    b, s, d = x.shape
    grid = (b, s // ts)
    tok = lambda w: pl.BlockSpec((1, ts, w), lambda bi, i: (bi, i, 0))
    feat = lambda r: pl.BlockSpec((1, r, ts), lambda bi, i: (bi, 0, i))
    full = lambda a: pl.BlockSpec(a.shape, lambda bi, i: (0,) * a.ndim)
    out_shape = (
        jax.ShapeDtypeStruct((b, s, 1024), BF16),
        jax.ShapeDtypeStruct((s, b * S5_WIDTH), F32),
        jax.ShapeDtypeStruct((b, A_WIDTH, s), BF16),
        jax.ShapeDtypeStruct((b, s, A_HD), BF16),
        jax.ShapeDtypeStruct((b, A_HD, s), BF16),
        jax.ShapeDtypeStruct((b, IDX_HEADS * IDX_DIM, s), BF16),
        jax.ShapeDtypeStruct((b, s, IDX_DIM), BF16),
        jax.ShapeDtypeStruct((b, IDX_HEADS, s), F32),
        jax.ShapeDtypeStruct((b, C_HEADS * C_QK_PAD, s), BF16),
        jax.ShapeDtypeStruct((b, C_HEADS, s, C_QK_PAD), BF16),
        jax.ShapeDtypeStruct((b, C_WIDTH, s), BF16),
    )
    out_specs = (
        tok(1024),
        pl.BlockSpec((ts, S5_WIDTH), lambda bi, i: (i, bi)),
        feat(A_WIDTH), tok(A_HD), feat(A_HD),
        feat(IDX_HEADS * IDX_DIM), tok(IDX_DIM), feat(IDX_HEADS),
        feat(C_HEADS * C_QK_PAD),
        pl.BlockSpec((1, C_HEADS, ts, C_QK_PAD), lambda bi, i: (bi, 0, i, 0)),
        feat(C_WIDTH),
    )
    in_specs = [tok(d), full(ng), full(wn), full(wt), full(gcol),
                pl.BlockSpec((R_END, ts), lambda bi, i: (0, i)), full(wuq), full(wukv)]
    return pl.pallas_call(
        _prep_kernel, grid=grid, in_specs=in_specs, out_specs=out_specs, out_shape=out_shape,
        compiler_params=_cparams(("parallel", "parallel")), name="prep",
    )(x, ng, wn, wt, gcol, rope, wuq, wukv)


def _chunk_causal_bias(tk, tq):
    kc = lax.broadcasted_iota(I32, (tk, tq), 0) // CHUNK
    qc = lax.broadcasted_iota(I32, (tk, tq), 1) // CHUNK
    return jnp.where(kc <= qc, 0.0, NEG_INF).astype(F32)


def _col_reduce8(x, op):
    tk, tq = x.shape
    x = x.reshape(tk // 8, 8, tq)
    return op(x, axis=0)


def _softmax_pv(nt, tk, k_tile, q_t, vt_tile, bias_tile, s_ref):
    tq = q_t.shape[1]

    def p1(kt, m8):
        r0 = pl.multiple_of(kt * tk, tk)
        s = jnp.dot(k_tile(r0), q_t, preferred_element_type=F32) + bias_tile(kt, r0)
        s_ref[pl.ds(r0, tk), :] = s
        return jnp.maximum(m8, _col_reduce8(s, jnp.max))

    m8 = lax.fori_loop(0, nt, p1, jnp.full((8, tq), NEG_INF, F32))
    m = jnp.max(m8, axis=0, keepdims=True)

    dv = vt_tile(0).shape[0]

    def p2(kt, carry):
        l8, acc = carry
        r0 = pl.multiple_of(kt * tk, tk)
        p = jnp.exp(s_ref[pl.ds(r0, tk), :] - m)
        l8 = l8 + _col_reduce8(p, jnp.sum)
        acc = acc + jnp.dot(vt_tile(r0), p.astype(BF16), preferred_element_type=F32)
        return l8, acc

    l8, acc = lax.fori_loop(0, nt, p2, (jnp.zeros((8, tq), F32), jnp.zeros((dv, tq), F32)))
    return acc / jnp.sum(l8, axis=0, keepdims=True)


def _dsa_kernel(iqt_ref, iwt_ref, ik_ref, qt_ref, k_ref, vt_ref, ya_out,
                keys_ref, bias_ref, s_ref, ot_ref, thr_ref, jcut_ref, *, tq, seq, ktop):
    tk = tq
    j = pl.program_id(1)
    nt = j + 1

    def score_keys(r0):
        ikt = ik_ref[0, pl.ds(r0, tk), :]
        acc = jnp.zeros((tk, tq), F32)
        for hd in range(IDX_HEADS):
            sh = jnp.dot(ikt, iqt_ref[0, hd * IDX_DIM:(hd + 1) * IDX_DIM, :],
                         preferred_element_type=F32)
            acc = acc + jnp.maximum(sh, 0.0) * iwt_ref[0, hd:hd + 1, :]
        bits = lax.bitcast_convert_type(acc, I32)
        return bits ^ ((bits >> 31) & np.int32(0x7FFFFFFF))

    def idx_tile(kt, c):
        r0 = pl.multiple_of(kt * tk, tk)
        keys_ref[pl.ds(r0, tk), :] = score_keys(r0)
        return c

    lax.fori_loop(0, j, idx_tile, 0)
    rd = pl.multiple_of(j * tk, tk)
    kc = lax.broadcasted_iota(I32, (tk, tq), 0) // CHUNK
    qc = lax.broadcasted_iota(I32, (tk, tq), 1) // CHUNK
    keys_ref[pl.ds(rd, tk), :] = jnp.where(kc <= qc, score_keys(rd), INT_MIN)

    def count(pred):
        def body(kt, c8):
            r0 = pl.multiple_of(kt * tk, tk)
            kk = keys_ref[pl.ds(r0, tk), :]
            row = r0 + lax.broadcasted_iota(I32, (tk, tq), 0)
            return c8 + _col_reduce8(pred(kk, row).astype(I32), jnp.sum)
        c8 = lax.fori_loop(0, nt, body, jnp.zeros((8, tq), I32))
        return jnp.sum(c8, axis=0, keepdims=True)

    @pl.when(nt * tk <= ktop)
    def _dense():
        thr_ref[...] = jnp.full((1, tq), INT_MIN, I32)
        jcut_ref[...] = jnp.zeros((1, tq), I32)

    @pl.when(nt * tk > ktop)
    def _topk():
        def bit_step(i, thr_u):
            cand_u = thr_u | lax.shift_left(np.int32(1), 31 - i)
            cand = cand_u ^ INT_MIN
            cnt = count(lambda kk, row: kk >= cand)
            return jnp.where(cnt >= ktop, cand_u, thr_u)

        thr = lax.fori_loop(0, 32, bit_step, jnp.zeros((1, tq), I32)) ^ INT_MIN
        thr_ref[...] = thr
        jcut_ref[...] = jnp.full((1, tq), seq, I32)
        n_ge = count(lambda kk, row: kk >= thr)

        @pl.when(jnp.max(n_ge) > ktop)
        def _ties():
            need = ktop - count(lambda kk, row: kk > thr)
            nbits = int(seq - 1).bit_length()

            def row_step(i, jp):
                cand = jp | lax.shift_left(np.int32(1), nbits - 1 - i)
                c = count(lambda kk, row: (kk == thr) & (row < cand))
                return jnp.where(c < need, cand, jp)

            jcut_ref[...] = lax.fori_loop(0, nbits, row_step, jnp.zeros((1, tq), I32)) + 1

    thr = thr_ref[...]
    jcut = jcut_ref[...]

    def bias_fill(kt, c):
        r0 = pl.multiple_of(kt * tk, tk)
        kk = keys_ref[pl.ds(r0, tk), :]
        row = r0 + lax.broadcasted_iota(I32, (tk, tq), 0)
        sel = (kk > thr) | ((kk == thr) & (row < jcut))
        bias_ref[pl.ds(r0, tk), :] = jnp.where(sel, 0.0, NEG_INF).astype(F32)
        return c

    lax.fori_loop(0, nt, bias_fill, 0)

    for hd in range(A_HEADS):
        ot_ref[hd * A_HD:(hd + 1) * A_HD, :] = _softmax_pv(
            nt, tk,
            lambda r0: k_ref[0, pl.ds(r0, tk), :],
            qt_ref[0, hd * A_HD:(hd + 1) * A_HD, :],
            lambda r0: vt_ref[0, :, pl.ds(r0, tk)],
            lambda kt, r0: bias_ref[pl.ds(r0, tk), :],
            s_ref)
    ya_out[0] = ot_ref[...].T.astype(BF16)


def _dsa_call(iqt, iwt, ik, qt, k, vt, tq):
    b, _, s = qt.shape
    ktop = min(TOPK_MAX, s // 4)
    assert s % tq == 0 and tq % CHUNK == 0 and ktop % tq == 0
    grid = (b, s // tq)
    qblk = lambda r: pl.BlockSpec((1, r, tq), lambda bi, j: (bi, 0, j))
    in_specs = [
        qblk(IDX_HEADS * IDX_DIM), qblk(IDX_HEADS),
        pl.BlockSpec((1, s, IDX_DIM), lambda bi, j: (bi, 0, 0)),
        qblk(A_WIDTH),
        pl.BlockSpec((1, s, A_HD), lambda bi, j: (bi, 0, 0)),
        pl.BlockSpec((1, A_HD, s), lambda bi, j: (bi, 0, 0)),
    ]
    return pl.pallas_call(
        functools.partial(_dsa_kernel, tq=tq, seq=s, ktop=ktop),
        grid=grid, in_specs=in_specs,
        out_specs=pl.BlockSpec((1, tq, A_WIDTH), lambda bi, j: (bi, j, 0)),
        out_shape=jax.ShapeDtypeStruct((b, s, A_WIDTH), BF16),
        scratch_shapes=[pltpu.VMEM((s, tq), I32), pltpu.VMEM((s, tq), F32), pltpu.VMEM((s, tq), F32),
                        pltpu.VMEM((A_WIDTH, tq), F32), pltpu.VMEM((1, tq), I32), pltpu.VMEM((1, tq), I32)],
        compiler_params=_cparams(("parallel", "arbitrary")), name="dsa",
    )(iqt, iwt, ik, qt, k, vt)


def _mla_kernel(cqt_ref, ck_ref, cvt_ref, yc_out, s_ref, ot_ref, *, tq):
    tk = tq
    j = pl.program_id(1)
    nt = j + 1
    diag = _chunk_causal_bias(tk, tq)
    for hd in range(C_HEADS):
        ot_ref[hd * C_VDIM:(hd + 1) * C_VDIM, :] = _softmax_pv(
            nt, tk,
            lambda r0: ck_ref[0, hd, pl.ds(r0, tk), :],
            cqt_ref[0, hd * C_QK_PAD:(hd + 1) * C_QK_PAD, :],
            lambda r0: cvt_ref[0, hd * C_VDIM:(hd + 1) * C_VDIM, pl.ds(r0, tk)],
            lambda kt, r0: jnp.where(kt == j, diag, 0.0),
            s_ref)
    yc_out[0] = ot_ref[...].T.astype(BF16)


def _mla_call(cqt, ck, cvt, tq):
    b, _, s = cqt.shape
    assert s % tq == 0 and tq % CHUNK == 0
    grid = (b, s // tq)
    in_specs = [
        pl.BlockSpec((1, C_HEADS * C_QK_PAD, tq), lambda bi, j: (bi, 0, j)),
        pl.BlockSpec((1, C_HEADS, s, C_QK_PAD), lambda bi, j: (bi, 0, 0, 0)),
        pl.BlockSpec((1, C_WIDTH, s), lambda bi, j: (bi, 0, 0)),
    ]
    return pl.pallas_call(
        functools.partial(_mla_kernel, tq=tq),
        grid=grid, in_specs=in_specs,
        out_specs=pl.BlockSpec((1, tq, C_WIDTH), lambda bi, j: (bi, j, 0)),
        out_shape=jax.ShapeDtypeStruct((b, s, C_WIDTH), BF16),
        scratch_shapes=[pltpu.VMEM((s, tq), F32), pltpu.VMEM((C_WIDTH, tq), F32)],
        compiler_params=_cparams(("parallel", "arbitrary")), name="mla",
    )(cqt, ck, cvt)


def _s5_kernel(u_ref, bblk_ref, cre_ref, cim_ref, lam_ref, dskip_ref, wglu_ref, yb_out,
               state_ref, bb_ref, xs_ref, *, nb, tsteps):
    @pl.when(pl.program_id(0) == 0)
    def _init():
        state_ref[...] = jnp.zeros_like(state_ref)

    ar, ai, step = lam_ref[0:1, :], lam_ref[1:2, :], jnp.exp(lam_ref[2:3, :])
    mag = jnp.exp(ar * step)
    abar_re = mag * jnp.cos(ai * step)
    abar_im = mag * jnp.sin(ai * step)
    den = ar * ar + ai * ai
    nr = abar_re - 1.0
    f_re = (nr * ar + abar_im * ai) / den
    f_im = (abar_im * ar - nr * ai) / den

    u = u_ref[...]
    bu = jnp.dot(u.astype(BF16), bblk_ref[...], preferred_element_type=F32)
    bu_re, bu_im = bu[:, :S5_CH], bu[:, S5_CH:]
    bb_ref[:, :S5_CH] = f_re * bu_re - f_im * bu_im
    bb_ref[:, S5_CH:] = f_re * bu_im + f_im * bu_re

    a_re = jnp.broadcast_to(abar_re, (nb, S5_CH))
    a_im = jnp.broadcast_to(abar_im, (nb, S5_CH))

    def scan_step(t, carry):
        xr, xi = carry
        r0 = pl.multiple_of(t * nb, nb)
        nxr = a_re * xr - a_im * xi + bb_ref[pl.ds(r0, nb), :S5_CH]
        nxi = a_re * xi + a_im * xr + bb_ref[pl.ds(r0, nb), S5_CH:]
        xs_ref[pl.ds(r0, nb), :S5_CH] = nxr
        xs_ref[pl.ds(r0, nb), S5_CH:] = nxi
        return nxr, nxi

    xr, xi = lax.fori_loop(0, tsteps, scan_step, (state_ref[:, :S5_CH], state_ref[:, S5_CH:]))
    state_ref[:, :S5_CH] = xr
    state_ref[:, S5_CH:] = xi

    y = (jnp.dot(xs_ref[:, :S5_CH].astype(BF16), cre_ref[...], preferred_element_type=F32)
         - jnp.dot(xs_ref[:, S5_CH:].astype(BF16), cim_ref[...], preferred_element_type=F32)
         + dskip_ref[...] * u)
    g = 0.5 * y * (1.0 + jnp.tanh(np.float32(np.sqrt(2.0 / np.pi)) * (y + 0.044715 * (y * y * y))))
    gate = jax.nn.sigmoid(jnp.dot(g.astype(BF16), wglu_ref[...], preferred_element_type=F32))
    yb_out[...] = (g * gate).astype(BF16)


def _s5_call(u2d, bblk, cre, cim, lam, dskip, wglu, nb, tsteps):
    rows = u2d.shape[0]
    blk = nb * tsteps
    assert rows % blk == 0 and nb % 8 == 0
    full = lambda a: pl.BlockSpec(a.shape, lambda i: (0,) * a.ndim)
    return pl.pallas_call(
        functools.partial(_s5_kernel, nb=nb, tsteps=tsteps),
        grid=(rows // blk,),
        in_specs=[pl.BlockSpec((blk, S5_WIDTH), lambda i: (i, 0)),
                  full(bblk), full(cre), full(cim), full(lam), full(dskip), full(wglu)],
        out_specs=pl.BlockSpec((blk, S5_WIDTH), lambda i: (i, 0)),
        out_shape=jax.ShapeDtypeStruct((rows, S5_WIDTH), BF16),
        scratch_shapes=[pltpu.VMEM((nb, 2 * S5_CH), F32), pltpu.VMEM((blk, 2 * S5_CH), F32),
                        pltpu.VMEM((blk, 2 * S5_CH), F32)],
        compiler_params=_cparams(("arbitrary",)), name="s5",
    )(u2d, bblk, cre, cim, lam, dskip, wglu)


def _out_kernel(x_ref, g_ref, ya_ref, yb_ref, yc_ref, wo_ref, o_ref):
    g = g_ref[0].astype(F32)
    sg = g * jax.nn.sigmoid(g)
    ma = (ya_ref[0].astype(F32) * sg[:, :A_WIDTH]).astype(BF16)
    mb = (yb_ref[...].astype(F32) * sg[:, A_WIDTH:A_WIDTH + S5_WIDTH]).astype(BF16)
    mc = (yc_ref[0].astype(F32) * sg[:, A_WIDTH + S5_WIDTH:]).astype(BF16)
    o_ref[0] = (x_ref[0]
                + jnp.dot(ma, wo_ref[:A_WIDTH, :], preferred_element_type=F32)
                + jnp.dot(mb, wo_ref[A_WIDTH:A_WIDTH + S5_WIDTH, :], preferred_element_type=F32)
                + jnp.dot(mc, wo_ref[A_WIDTH + S5_WIDTH:, :], preferred_element_type=F32))


def _out_call(x, g, ya, yb, yc, wo, ts):
    b, s, d = x.shape
    tok = lambda w: pl.BlockSpec((1, ts, w), lambda bi, i: (bi, i, 0))
    return pl.pallas_call(
        _out_kernel, grid=(b, s // ts),
        in_specs=[tok(d), tok(1024), tok(A_WIDTH),
                  pl.BlockSpec((ts, S5_WIDTH), lambda bi, i: (i, bi)), tok(C_WIDTH),
                  pl.BlockSpec(wo.shape, lambda bi, i: (0, 0))],
        out_specs=tok(d), out_shape=jax.ShapeDtypeStruct((b, s, d), F32),
        compiler_params=_cparams(("parallel", "parallel")), name="outproj",
    )(x, g, ya, yb, yc, wo)


def _rope_table(s):
    pos = jnp.arange(s, dtype=jnp.int32).astype(F32)

    def cs(d):
        half = d // 2
        inv = ROPE_THETA ** (-jnp.arange(half, dtype=F32) * 2.0 / d)
        ang = pos[:, None] * inv[None, :]
        return jnp.cos(ang).T, jnp.sin(ang).T

    c64, s64 = cs(A_HD)
    c32, s32 = cs(IDX_DIM)
    return jnp.concatenate([c64, s64, c32, s32], axis=0)


def _block_diag(w):
    g, r, c = w.shape
    eye = jnp.eye(g, dtype=w.dtype)
    return (w[:, :, None, :] * eye[:, None, :, None]).reshape(g * r, g * c)


def _layer_operands(l, norm_g, w_in, attn_q_norm, attn_k_norm, mla_q_lora_norm, mla_kv_lora_norm,
                    mla_w_uq, mla_w_ukv, mla_q_norm, mla_k_norm, ssm_a_re, ssm_a_im, ssm_b_re,
                    ssm_b_im, ssm_c_re, ssm_c_im, ssm_d, ssm_log_step, ssm_w_glu, w_out):
    pts = [int(v) for v in np.cumsum(SPLIT_SIZES)[:-1]]
    (qa, ka, va, iq, ik, iw, ga, u, gb, cq, ckv, kpe, gc) = jnp.split(w_in[l], pts, axis=-1)
    wn = jnp.concatenate([ga, gb, gc, u], axis=1).astype(BF16)
    pad = jnp.zeros((w_in.shape[1], T_END - T_IW - IDX_HEADS), w_in.dtype)
    wt = jnp.concatenate([qa, ka, va, iq, ik, cq, ckv, kpe, iw, pad], axis=1).T.astype(BF16)
    gcol = jnp.concatenate([attn_q_norm[l], attn_k_norm[l], mla_q_lora_norm[l], mla_kv_lora_norm[l],
                            mla_q_norm[l], mla_k_norm[l]]).astype(F32)[:, None]
    wuq = mla_w_uq[l].T.reshape(C_HEADS, C_QK, Q_LORA)
    wuq = jnp.pad(wuq, ((0, 0), (0, C_QK_PAD - C_QK), (0, 0))).reshape(C_HEADS * C_QK_PAD, Q_LORA).astype(BF16)
    wukv = mla_w_ukv[l].T.astype(BF16)
    bblk = jnp.concatenate([_block_diag(jnp.swapaxes(ssm_b_re[l], 1, 2)),
                            _block_diag(jnp.swapaxes(ssm_b_im[l], 1, 2))], axis=1).astype(BF16)
    cre = _block_diag(jnp.swapaxes(ssm_c_re[l], 1, 2)).astype(BF16)
    cim = _block_diag(jnp.swapaxes(ssm_c_im[l], 1, 2)).astype(BF16)
    lam = jnp.stack([ssm_a_re[l].reshape(-1), ssm_a_im[l].reshape(-1),
                     jnp.repeat(ssm_log_step[l], S5_STATE)]).astype(F32)
    lam = jnp.concatenate([lam, jnp.zeros((5, S5_CH), F32)], axis=0)
    dskip = ssm_d[l].reshape(1, S5_WIDTH).astype(F32)
    return dict(ng=norm_g[l][None, :].astype(F32), wn=wn, wt=wt, gcol=gcol, wuq=wuq, wukv=wukv,
                bblk=bblk, cre=cre, cim=cim, lam=lam, dskip=dskip, wglu=ssm_w_glu[l].astype(BF16),
                wo=w_out[l].astype(BF16))


def kernel(x, norm_g, w_in, attn_q_norm, attn_k_norm, mla_q_lora_norm, mla_kv_lora_norm, mla_w_uq, mla_w_ukv, mla_q_norm, mla_k_norm, ssm_a_re, ssm_a_im, ssm_b_re, ssm_b_im, ssm_c_re, ssm_c_im, ssm_d, ssm_log_step, ssm_w_glu, w_out):
    b, s, _ = x.shape
    ts = min(512, s)
    tq = 256
    s5_steps = 32
    rope = _rope_table(s)
    params = (norm_g, w_in, attn_q_norm, attn_k_norm, mla_q_lora_norm, mla_kv_lora_norm, mla_w_uq,
              mla_w_ukv, mla_q_norm, mla_k_norm, ssm_a_re, ssm_a_im, ssm_b_re, ssm_b_im, ssm_c_re,
              ssm_c_im, ssm_d, ssm_log_step, ssm_w_glu, w_out)
    for l in range(w_in.shape[0]):
        p = _layer_operands(l, *params)
        (g, u, qt, k, vt, iqt, ik, iwt, cqt, ck, cvt) = _prep_call(
            x, p["ng"], p["wn"], p["wt"], p["gcol"], rope, p["wuq"], p["wukv"], ts)
        ya = _dsa_call(iqt, iwt, ik, qt, k, vt, tq)
        yc = _mla_call(cqt, ck, cvt, tq)
        yb = _s5_call(u.reshape(s * b, S5_WIDTH), p["bblk"], p["cre"], p["cim"], p["lam"],
                      p["dskip"], p["wglu"], b, s5_steps)
        x = _out_call(x, g, ya, yb.reshape(s, b * S5_WIDTH), yc, p["wo"], ts)
    return x
```

```python
import functools

import numpy as np
import jax
import jax.numpy as jnp
from jax import lax
from jax.experimental import pallas as pl
from jax.experimental.pallas import tpu as pltpu

F32 = jnp.float32
BF16 = jnp.bfloat16
I32 = jnp.int32
I16 = jnp.int16

TILE_UNROLL = 2

CHUNK = 64
ROPE_THETA = 10000.0
EPS = 1e-6

A_HEADS = 6
A_HD = 64
A_WIDTH = A_HEADS * A_HD
IDX_HEADS = 8
IDX_DIM = 32
TOPK_MAX = 256

S5_GROUP = 16
S5_WIDTH = 256
S5_GROUPS = S5_WIDTH // S5_GROUP
S5_STATE = 64
S5_CH = S5_GROUPS * S5_STATE

C_HEADS = 6
C_NOPE = 64
C_ROPE = 32
C_VDIM = 64
C_QK = C_NOPE + C_ROPE
C_QK_PAD = 128
C_WIDTH = C_HEADS * C_VDIM
Q_LORA = 256
KV_LORA = 128

SPLIT_SIZES = (A_WIDTH, A_HD, A_HD, IDX_HEADS * IDX_DIM, IDX_DIM, IDX_HEADS, A_WIDTH,
               S5_WIDTH, S5_WIDTH, Q_LORA, KV_LORA, C_ROPE, C_WIDTH)

T_QA, T_KA, T_VA, T_IQ, T_IK, T_CQ, T_CKV, T_KPE, T_IW, T_END = (
    0, 384, 448, 512, 768, 800, 1056, 1184, 1216, 1232)
G_AQ, G_AK, G_QL, G_KVL, G_MQ, G_MK, G_END = 0, 64, 128, 384, 512, 608, 704
R_C64, R_S64, R_C32, R_S32, R_END = 0, 32, 64, 80, 96

INT_MIN = np.int32(-2 ** 31)
NEG_INF = float("-inf")

VMEM_LIMIT = 56 * 1024 * 1024


def _cparams(sem):
    return pltpu.CompilerParams(dimension_semantics=sem, vmem_limit_bytes=VMEM_LIMIT)


def _rms_rows(x, gcol):
    r = lax.rsqrt(jnp.mean(x * x, axis=0, keepdims=True) + EPS)
    return x * r * gcol


def _rope_rows(x, cos, sin):
    half = x.shape[0] // 2
    x1, x2 = x[:half], x[half:]
    return jnp.concatenate([x1 * cos - x2 * sin, x2 * cos + x1 * sin], axis=0)


def _rows_to_token_major(xt, width):
    d, t = xt.shape
    if d < 128:
        xt = jnp.concatenate([xt, jnp.zeros((128 - d, t), F32)], axis=0)
    return xt.T[:, :width]


def _prep_kernel(x_ref, ng_ref, wn_ref, wt_ref, gcol_ref, rope_ref, wuq_ref, wukv_ref,
                 g_out, u_out, qt_out, k_out, vt_out, iqt_out, ik_out, iwt_out,
                 cqt_out, ck_out, cvt_out):
    x = x_ref[0]
    ms = jnp.mean(x * x, axis=-1, keepdims=True)
    h = (x * lax.rsqrt(ms + EPS) * ng_ref[...]).astype(BF16)

    pn = jnp.dot(h, wn_ref[...], preferred_element_type=F32)
    g_out[0] = pn[:, :1024].astype(BF16)
    u_out[...] = pn[:, 1024:1280]

    def tdot(lo, hi):
        return lax.dot_general(wt_ref[lo:hi, :], h, (((1,), (1,)), ((), ())),
                               preferred_element_type=F32)

    cos64 = rope_ref[R_C64:R_S64, :]
    sin64 = rope_ref[R_S64:R_C32, :]
    cos32 = rope_ref[R_C32:R_S32, :]
    sin32 = rope_ref[R_S32:R_END, :]

    pa = tdot(T_QA, T_IQ)
    g_aq = gcol_ref[G_AQ:G_AK, :]
    for hd in range(A_HEADS):
        qh = _rope_rows(_rms_rows(pa[hd * A_HD:(hd + 1) * A_HD], g_aq), cos64, sin64)
        qt_out[0, hd * A_HD:(hd + 1) * A_HD, :] = (qh * (A_HD ** -0.5)).astype(BF16)
    kt = _rope_rows(_rms_rows(pa[T_KA:T_VA], gcol_ref[G_AK:G_QL, :]), cos64, sin64)
    k_out[0] = _rows_to_token_major(kt, A_HD).astype(BF16)
    vt_out[0] = pa[T_VA:T_IQ].astype(BF16)

    pb = tdot(T_IQ, T_CQ)
    for hd in range(IDX_HEADS):
        iqh = _rope_rows(pb[hd * IDX_DIM:(hd + 1) * IDX_DIM], cos32, sin32)
        iqt_out[0, hd * IDX_DIM:(hd + 1) * IDX_DIM, :] = iqh.astype(BF16)
    ikt = _rope_rows(pb[T_IK - T_IQ:T_CQ - T_IQ], cos32, sin32)
    ik_out[0] = _rows_to_token_major(ikt, IDX_DIM).astype(BF16)

    pc = tdot(T_CQ, T_END)
    iwt_out[0] = pc[T_IW - T_CQ:T_IW - T_CQ + IDX_HEADS] * ((IDX_HEADS ** -0.5) * (IDX_DIM ** -0.5))

    cqn = _rms_rows(pc[:Q_LORA], gcol_ref[G_QL:G_KVL, :]).astype(BF16)
    qt = jnp.dot(wuq_ref[...], cqn, preferred_element_type=F32)
    g_mq = gcol_ref[G_MQ:G_MK, :]
    zpad = jnp.zeros((C_QK_PAD - C_QK, x.shape[0]), F32)
    for hd in range(C_HEADS):
        qh = _rms_rows(qt[hd * C_QK_PAD:hd * C_QK_PAD + C_QK], g_mq)
        qh = jnp.concatenate([qh[:C_NOPE], _rope_rows(qh[C_NOPE:], cos32, sin32), zpad], axis=0)
        cqt_out[0, hd * C_QK_PAD:(hd + 1) * C_QK_PAD, :] = (qh * (C_QK ** -0.5)).astype(BF16)

    ckvn = _rms_rows(pc[T_CKV - T_CQ:T_KPE - T_CQ], gcol_ref[G_KVL:G_MQ, :]).astype(BF16)
    kvt = jnp.dot(wukv_ref[...], ckvn, preferred_element_type=F32)
    kpe = pc[T_KPE - T_CQ:T_IW - T_CQ]
    kpe_ss = jnp.sum(kpe * kpe, axis=0, keepdims=True)
    g_mk = gcol_ref[G_MK:G_END, :]
    for hd in range(C_HEADS):
        kn = kvt[hd * 128:hd * 128 + C_NOPE]
        ss = jnp.sum(kn * kn, axis=0, keepdims=True) + kpe_ss
        r = lax.rsqrt(ss * (1.0 / C_QK) + EPS)
        kh = jnp.concatenate([kn * r * g_mk[:C_NOPE],
                              _rope_rows(kpe * r * g_mk[C_NOPE:], cos32, sin32), zpad], axis=0)
        ck_out[0, hd] = kh.T.astype(BF16)
        cvt_out[0, hd * C_VDIM:(hd + 1) * C_VDIM, :] = kvt[hd * 128 + C_NOPE:(hd + 1) * 128].astype(BF16)


def _prep_call(x, ng, wn, wt, gcol, rope, wuq, wukv, ts):
    b, s, d = x.shape
    grid = (b, s // ts)
    tok = lambda w: pl.BlockSpec((1, ts, w), lambda bi, i: (bi, i, 0))
    feat = lambda r: pl.BlockSpec((1, r, ts), lambda bi, i: (bi, 0, i))
    full = lambda a: pl.BlockSpec(a.shape, lambda bi, i: (0,) * a.ndim)
    out_shape = (
        jax.ShapeDtypeStruct((b, s, 1024), BF16),
        jax.ShapeDtypeStruct((s, b * S5_WIDTH), F32),
        jax.ShapeDtypeStruct((b, A_WIDTH, s), BF16),
        jax.ShapeDtypeStruct((b, s, A_HD), BF16),
        jax.ShapeDtypeStruct((b, A_HD, s), BF16),
        jax.ShapeDtypeStruct((b, IDX_HEADS * IDX_DIM, s), BF16),
        jax.ShapeDtypeStruct((b, s, IDX_DIM), BF16),
        jax.ShapeDtypeStruct((b, IDX_HEADS, s), F32),
        jax.ShapeDtypeStruct((b, C_HEADS * C_QK_PAD, s), BF16),
        jax.ShapeDtypeStruct((b, C_HEADS, s, C_QK_PAD), BF16),
        jax.ShapeDtypeStruct((b, C_WIDTH, s), BF16),
    )
    out_specs = (
        tok(1024),
        pl.BlockSpec((ts, S5_WIDTH), lambda bi, i: (i, bi)),
        feat(A_WIDTH), tok(A_HD), feat(A_HD),
        feat(IDX_HEADS * IDX_DIM), tok(IDX_DIM), feat(IDX_HEADS),
        feat(C_HEADS * C_QK_PAD),
        pl.BlockSpec((1, C_HEADS, ts, C_QK_PAD), lambda bi, i: (bi, 0, i, 0)),
        feat(C_WIDTH),
    )
    in_specs = [tok(d), full(ng), full(wn), full(wt), full(gcol),
                pl.BlockSpec((R_END, ts), lambda bi, i: (0, i)), full(wuq), full(wukv)]
    return pl.pallas_call(
        _prep_kernel, grid=grid, in_specs=in_specs, out_specs=out_specs, out_shape=out_shape,
        compiler_params=_cparams(("parallel", "parallel")), name="prep",
    )(x, ng, wn, wt, gcol, rope, wuq, wukv)


def _chunk_causal_bias(tk, tq):
    kc = lax.broadcasted_iota(I32, (tk, tq), 0) // CHUNK
    qc = lax.broadcasted_iota(I32, (tk, tq), 1) // CHUNK
    return jnp.where(kc <= qc, 0.0, NEG_INF).astype(F32)


def _col_reduce8(x, op):
    tk, tq = x.shape
    x = x.reshape(tk // 8, 8, tq)
    return op(x, axis=0)


def _for_tiles(nt, tk, body, carry):
    nfull = nt // TILE_UNROLL

    def group(i, c):
        for k in range(TILE_UNROLL):
            c = body(pl.multiple_of((i * TILE_UNROLL + k) * tk, tk), c)
        return c

    carry = lax.fori_loop(0, nfull, group, carry)
    return lax.fori_loop(nfull * TILE_UNROLL, nt, lambda kt, c: body(pl.multiple_of(kt * tk, tk), c), carry)


def _softmax_pv(nt, tk, k_tile, q_t, vt_tile, s_ref, bias_tile=None, diag_bias=None):
    tq = q_t.shape[1]

    def scores(r0, m8, bias):
        s = jnp.dot(k_tile(r0), q_t, preferred_element_type=F32)
        if bias is not None:
            s = s + bias
        s_ref[pl.ds(r0, tk), :] = s
        return jnp.maximum(m8, _col_reduce8(s, jnp.max))

    m8 = _for_tiles(nt, tk, lambda r0, c: scores(r0, c, None if bias_tile is None else bias_tile(r0)),
                    jnp.full((8, tq), NEG_INF, F32))
    nt_all = nt
    if diag_bias is not None:
        m8 = scores(pl.multiple_of(nt * tk, tk), m8, diag_bias)
        nt_all = nt + 1
    m = jnp.max(m8, axis=0, keepdims=True)

    dv = vt_tile(0).shape[0]

    def p2(r0, carry):
        l8, acc = carry
        p = jnp.exp(s_ref[pl.ds(r0, tk), :] - m)
        l8 = l8 + _col_reduce8(p, jnp.sum)
        acc = acc + jnp.dot(vt_tile(r0), p.astype(BF16), preferred_element_type=F32)
        return l8, acc

    l8, acc = _for_tiles(nt_all, tk, p2, (jnp.zeros((8, tq), F32), jnp.zeros((dv, tq), F32)))
    return acc / jnp.sum(l8, axis=0, keepdims=True)


def _dsa_kernel(iqt_ref, iwt_ref, ik_ref, qt_ref, k_ref, vt_ref, ya_out,
                keys_ref, hi_ref, lo_ref, s_ref, ot_ref, thr_ref, jcut_ref, *, tq, seq, ktop):
    tk = tq
    j = pl.program_id(1)
    nt = j + 1

    def put_keys(r0, diagonal):
        ikt = ik_ref[0, pl.ds(r0, tk), :]
        acc = jnp.zeros((tk, tq), F32)
        for hd in range(IDX_HEADS):
            sh = jnp.dot(ikt, iqt_ref[0, hd * IDX_DIM:(hd + 1) * IDX_DIM, :],
                         preferred_element_type=F32)
            acc = acc + jnp.maximum(sh, 0.0) * iwt_ref[0, hd:hd + 1, :]
        bits = lax.bitcast_convert_type(acc, I32)
        key = bits ^ ((bits >> 31) & np.int32(0x7FFFFFFF))
        if diagonal:
            kc = lax.broadcasted_iota(I32, (tk, tq), 0) // CHUNK
            qc = lax.broadcasted_iota(I32, (tk, tq), 1) // CHUNK
            key = jnp.where(kc <= qc, key, INT_MIN)
        keys_ref[pl.ds(r0, tk), :] = key
        hi_ref[pl.ds(r0, tk), :] = (key >> 16).astype(I16)
        lo_ref[pl.ds(r0, tk), :] = ((key & np.int32(0xFFFF)) - 32768).astype(I16)

    def idx_tile(kt, c):
        put_keys(pl.multiple_of(kt * tk, tk), False)
        return c

    lax.fori_loop(0, j, idx_tile, 0)
    put_keys(pl.multiple_of(j * tk, tk), True)

    def count(pred):
        def body(r0, c8):
            kk = keys_ref[pl.ds(r0, tk), :]
            row = r0 + lax.broadcasted_iota(I32, (tk, tq), 0)
            return c8 + _col_reduce8(pred(kk, row).astype(I32), jnp.sum)
        c8 = _for_tiles(nt, tk, body, jnp.zeros((8, tq), I32))
        return jnp.sum(c8, axis=0, keepdims=True)

    def count16(ref, pred):
        def body(r0, c16):
            m = pred(ref[pl.ds(r0, tk), :]).astype(I16)
            for q in range(tk // 16):
                c16 = c16 + m[q * 16:(q + 1) * 16]
            return c16
        c16 = _for_tiles(nt, tk, body, jnp.zeros((16, tq), I16))
        return jnp.sum(c16.astype(I32), axis=0, keepdims=True)

    def radix16(ref, kth):
        def bit_step(i, u):
            cand = u | lax.shift_left(np.int32(1), 15 - i)
            cand16 = (cand - 32768).astype(I16)
            cnt = count16(ref, lambda v: v >= cand16)
            return jnp.where(cnt >= kth, cand, u)
        return lax.fori_loop(0, 16, bit_step, jnp.zeros((1, tq), I32))

    @pl.when(nt * tk <= ktop)
    def _dense():
        thr_ref[...] = jnp.full((1, tq), INT_MIN, I32)
        jcut_ref[...] = jnp.zeros((1, tq), I32)

    @pl.when(nt * tk > ktop)
    def _topk():
        hi_thr = radix16(hi_ref, ktop) - 32768
        hi_thr16 = hi_thr.astype(I16)
        n_above = count16(hi_ref, lambda v: v > hi_thr16)

        def mask_lo(r0, c):
            same = hi_ref[pl.ds(r0, tk), :] == hi_thr16
            lo_ref[pl.ds(r0, tk), :] = jnp.where(same, lo_ref[pl.ds(r0, tk), :], np.int16(-32768))
            return c

        _for_tiles(nt, tk, mask_lo, 0)
        lo_thr = radix16(lo_ref, ktop - n_above)
        qpos = j * tq + lax.broadcasted_iota(I32, (1, tq), 1)
        all_kept = (qpos // CHUNK + 1) * CHUNK <= ktop
        thr = jnp.where(all_kept, INT_MIN, lax.shift_left(hi_thr, 16) | lo_thr)
        thr_ref[...] = thr
        jcut_ref[...] = jnp.where(all_kept, 0, seq)
        n_ge = count(lambda kk, row: kk >= thr)

        @pl.when(jnp.max(jnp.where(all_kept, 0, n_ge)) > ktop)
        def _ties():
            need = ktop - count(lambda kk, row: kk > thr)
            nbits = int(seq - 1).bit_length()

            def row_step(i, jp):
                cand = jp | lax.shift_left(np.int32(1), nbits - 1 - i)
                c = count(lambda kk, row: (kk == thr) & (row < cand))
                return jnp.where(c < need, cand, jp)

            jcut = lax.fori_loop(0, nbits, row_step, jnp.zeros((1, tq), I32)) + 1
            jcut_ref[...] = jnp.where(all_kept, 0, jcut)

    thr = thr_ref[...]
    jcut = jcut_ref[...]

    def bias_fill(r0, c):
        kk = keys_ref[pl.ds(r0, tk), :]
        row = r0 + lax.broadcasted_iota(I32, (tk, tq), 0)
        sel = (kk > thr) | ((kk == thr) & (row < jcut))
        keys_ref[pl.ds(r0, tk), :] = lax.bitcast_convert_type(
            jnp.where(sel, 0.0, NEG_INF).astype(F32), I32)
        return c

    _for_tiles(nt, tk, bias_fill, 0)

    for hd in range(A_HEADS):
        ot_ref[hd * A_HD:(hd + 1) * A_HD, :] = _softmax_pv(
            nt, tk,
            lambda r0: k_ref[0, pl.ds(r0, tk), :],
            qt_ref[0, hd * A_HD:(hd + 1) * A_HD, :],
            lambda r0: vt_ref[0, :, pl.ds(r0, tk)],
            s_ref,
            bias_tile=lambda r0: lax.bitcast_convert_type(keys_ref[pl.ds(r0, tk), :], F32))
    ya_out[0] = ot_ref[...].T.astype(BF16)


def _dsa_call(iqt, iwt, ik, qt, k, vt, tq):
    b, _, s = qt.shape
    ktop = min(TOPK_MAX, s // 4)
    assert s % tq == 0 and tq % CHUNK == 0 and ktop % CHUNK == 0
    grid = (b, s // tq)
    qblk = lambda r: pl.BlockSpec((1, r, tq), lambda bi, j: (bi, 0, j))
    in_specs = [
        qblk(IDX_HEADS * IDX_DIM), qblk(IDX_HEADS),
        pl.BlockSpec((1, s, IDX_DIM), lambda bi, j: (bi, 0, 0)),
        qblk(A_WIDTH),
        pl.BlockSpec((1, s, A_HD), lambda bi, j: (bi, 0, 0)),
        pl.BlockSpec((1, A_HD, s), lambda bi, j: (bi, 0, 0)),
    ]
    return pl.pallas_call(
        functools.partial(_dsa_kernel, tq=tq, seq=s, ktop=ktop),
        grid=grid, in_specs=in_specs,
        out_specs=pl.BlockSpec((1, tq, A_WIDTH), lambda bi, j: (bi, j, 0)),
        out_shape=jax.ShapeDtypeStruct((b, s, A_WIDTH), BF16),
        scratch_shapes=[pltpu.VMEM((s, tq), I32), pltpu.VMEM((s, tq), I16), pltpu.VMEM((s, tq), I16),
                        pltpu.VMEM((s, tq), F32),
                        pltpu.VMEM((A_WIDTH, tq), F32), pltpu.VMEM((1, tq), I32), pltpu.VMEM((1, tq), I32)],
        compiler_params=_cparams(("parallel", "arbitrary")), name="dsa",
    )(iqt, iwt, ik, qt, k, vt)


def _mla_kernel(cqt_ref, ck_ref, cvt_ref, yc_out, s_ref, ot_ref, *, tq):
    tk = tq
    j = pl.program_id(1)
    diag = _chunk_causal_bias(tk, tq)
    for hd in range(C_HEADS):
        ot_ref[hd * C_VDIM:(hd + 1) * C_VDIM, :] = _softmax_pv(
            j, tk,
            lambda r0: ck_ref[0, hd, pl.ds(r0, tk), :],
            cqt_ref[0, hd * C_QK_PAD:(hd + 1) * C_QK_PAD, :],
            lambda r0: cvt_ref[0, hd * C_VDIM:(hd + 1) * C_VDIM, pl.ds(r0, tk)],
            s_ref, diag_bias=diag)
    yc_out[0] = ot_ref[...].T.astype(BF16)


def _mla_call(cqt, ck, cvt, tq):
    b, _, s = cqt.shape
    assert s % tq == 0 and tq % CHUNK == 0
    grid = (b, s // tq)
    in_specs = [
        pl.BlockSpec((1, C_HEADS * C_QK_PAD, tq), lambda bi, j: (bi, 0, j)),
        pl.BlockSpec((1, C_HEADS, s, C_QK_PAD), lambda bi, j: (bi, 0, 0, 0)),
        pl.BlockSpec((1, C_WIDTH, s), lambda bi, j: (bi, 0, 0)),
    ]
    return pl.pallas_call(
        functools.partial(_mla_kernel, tq=tq),
        grid=grid, in_specs=in_specs,
        out_specs=pl.BlockSpec((1, tq, C_WIDTH), lambda bi, j: (bi, j, 0)),
        out_shape=jax.ShapeDtypeStruct((b, s, C_WIDTH), BF16),
        scratch_shapes=[pltpu.VMEM((s, tq), F32), pltpu.VMEM((C_WIDTH, tq), F32)],
        compiler_params=_cparams(("parallel", "arbitrary")), name="mla",
    )(cqt, ck, cvt)


def _s5_kernel(u_ref, bblk_ref, cre_ref, cim_ref, lam_ref, dskip_ref, wglu_ref, yb_out,
               state_ref, bb_ref, xs_ref, *, nb, tsteps):
    @pl.when(pl.program_id(0) == 0)
    def _init():
        state_ref[...] = jnp.zeros_like(state_ref)

    ar, ai, step = lam_ref[0:1, :], lam_ref[1:2, :], jnp.exp(lam_ref[2:3, :])
    mag = jnp.exp(ar * step)
    abar_re = mag * jnp.cos(ai * step)
    abar_im = mag * jnp.sin(ai * step)
    den = ar * ar + ai * ai
    nr = abar_re - 1.0
    f_re = (nr * ar + abar_im * ai) / den
    f_im = (abar_im * ar - nr * ai) / den

    u = u_ref[...]
    bu = jnp.dot(u.astype(BF16), bblk_ref[...], preferred_element_type=F32)
    bu_re, bu_im = bu[:, :S5_CH], bu[:, S5_CH:]
    bb_ref[:, :S5_CH] = f_re * bu_re - f_im * bu_im
    bb_ref[:, S5_CH:] = f_re * bu_im + f_im * bu_re

    a_re = jnp.broadcast_to(abar_re, (nb, S5_CH))
    a_im = jnp.broadcast_to(abar_im, (nb, S5_CH))

    def scan_step(t, carry):
        xr, xi = carry
        r0 = pl.multiple_of(t * nb, nb)
        nxr = a_re * xr - a_im * xi + bb_ref[pl.ds(r0, nb), :S5_CH]
        nxi = a_re * xi + a_im * xr + bb_ref[pl.ds(r0, nb), S5_CH:]
        xs_ref[pl.ds(r0, nb), :S5_CH] = nxr
        xs_ref[pl.ds(r0, nb), S5_CH:] = nxi
        return nxr, nxi

    xr, xi = lax.fori_loop(0, tsteps, scan_step, (state_ref[:, :S5_CH], state_ref[:, S5_CH:]))
    state_ref[:, :S5_CH] = xr
    state_ref[:, S5_CH:] = xi

    y = (jnp.dot(xs_ref[:, :S5_CH].astype(BF16), cre_ref[...], preferred_element_type=F32)
         - jnp.dot(xs_ref[:, S5_CH:].astype(BF16), cim_ref[...], preferred_element_type=F32)
         + dskip_ref[...] * u)
    g = 0.5 * y * (1.0 + jnp.tanh(np.float32(np.sqrt(2.0 / np.pi)) * (y + 0.044715 * (y * y * y))))
    gate = jax.nn.sigmoid(jnp.dot(g.astype(BF16), wglu_ref[...], preferred_element_type=F32))
    yb_out[...] = (g * gate).astype(BF16)


def _s5_call(u2d, bblk, cre, cim, lam, dskip, wglu, nb, tsteps):
    rows = u2d.shape[0]
    blk = nb * tsteps
    assert rows % blk == 0 and nb % 8 == 0
    full = lambda a: pl.BlockSpec(a.shape, lambda i: (0,) * a.ndim)
    return pl.pallas_call(
        functools.partial(_s5_kernel, nb=nb, tsteps=tsteps),
        grid=(rows // blk,),
        in_specs=[pl.BlockSpec((blk, S5_WIDTH), lambda i: (i, 0)),
                  full(bblk), full(cre), full(cim), full(lam), full(dskip), full(wglu)],
        out_specs=pl.BlockSpec((blk, S5_WIDTH), lambda i: (i, 0)),
        out_shape=jax.ShapeDtypeStruct((rows, S5_WIDTH), BF16),
        scratch_shapes=[pltpu.VMEM((nb, 2 * S5_CH), F32), pltpu.VMEM((blk, 2 * S5_CH), F32),
                        pltpu.VMEM((blk, 2 * S5_CH), F32)],
        compiler_params=_cparams(("arbitrary",)), name="s5",
    )(u2d, bblk, cre, cim, lam, dskip, wglu)


def _out_kernel(x_ref, g_ref, ya_ref, yb_ref, yc_ref, wo_ref, o_ref):
    g = g_ref[0].astype(F32)
    sg = g * jax.nn.sigmoid(g)
    ma = (ya_ref[0].astype(F32) * sg[:, :A_WIDTH]).astype(BF16)
    mb = (yb_ref[...].astype(F32) * sg[:, A_WIDTH:A_WIDTH + S5_WIDTH]).astype(BF16)
    mc = (yc_ref[0].astype(F32) * sg[:, A_WIDTH + S5_WIDTH:]).astype(BF16)
    o_ref[0] = (x_ref[0]
                + jnp.dot(ma, wo_ref[:A_WIDTH, :], preferred_element_type=F32)
                + jnp.dot(mb, wo_ref[A_WIDTH:A_WIDTH + S5_WIDTH, :], preferred_element_type=F32)
                + jnp.dot(mc, wo_ref[A_WIDTH + S5_WIDTH:, :], preferred_element_type=F32))


def _out_call(x, g, ya, yb, yc, wo, ts):
    b, s, d = x.shape
    tok = lambda w: pl.BlockSpec((1, ts, w), lambda bi, i: (bi, i, 0))
    return pl.pallas_call(
        _out_kernel, grid=(b, s // ts),
        in_specs=[tok(d), tok(1024), tok(A_WIDTH),
                  pl.BlockSpec((ts, S5_WIDTH), lambda bi, i: (i, bi)), tok(C_WIDTH),
                  pl.BlockSpec(wo.shape, lambda bi, i: (0, 0))],
        out_specs=tok(d), out_shape=jax.ShapeDtypeStruct((b, s, d), F32),
        compiler_params=_cparams(("parallel", "parallel")), name="outproj",
    )(x, g, ya, yb, yc, wo)


def _rope_table(s):
    pos = jnp.arange(s, dtype=jnp.int32).astype(F32)

    def cs(d):
        half = d // 2
        inv = ROPE_THETA ** (-jnp.arange(half, dtype=F32) * 2.0 / d)
        ang = pos[:, None] * inv[None, :]
        return jnp.cos(ang).T, jnp.sin(ang).T

    c64, s64 = cs(A_HD)
    c32, s32 = cs(IDX_DIM)
    return jnp.concatenate([c64, s64, c32, s32], axis=0)


def _block_diag(w):
    g, r, c = w.shape
    eye = jnp.eye(g, dtype=w.dtype)
    return (w[:, :, None, :] * eye[:, None, :, None]).reshape(g * r, g * c)


def _layer_operands(l, norm_g, w_in, attn_q_norm, attn_k_norm, mla_q_lora_norm, mla_kv_lora_norm,
                    mla_w_uq, mla_w_ukv, mla_q_norm, mla_k_norm, ssm_a_re, ssm_a_im, ssm_b_re,
                    ssm_b_im, ssm_c_re, ssm_c_im, ssm_d, ssm_log_step, ssm_w_glu, w_out):
    pts = [int(v) for v in np.cumsum(SPLIT_SIZES)[:-1]]
    (qa, ka, va, iq, ik, iw, ga, u, gb, cq, ckv, kpe, gc) = jnp.split(w_in[l], pts, axis=-1)
    wn = jnp.concatenate([ga, gb, gc, u], axis=1).astype(BF16)
    pad = jnp.zeros((w_in.shape[1], T_END - T_IW - IDX_HEADS), w_in.dtype)
    wt = jnp.concatenate([qa, ka, va, iq, ik, cq, ckv, kpe, iw, pad], axis=1).T.astype(BF16)
    gcol = jnp.concatenate([attn_q_norm[l], attn_k_norm[l], mla_q_lora_norm[l], mla_kv_lora_norm[l],
                            mla_q_norm[l], mla_k_norm[l]]).astype(F32)[:, None]
    wuq = mla_w_uq[l].T.reshape(C_HEADS, C_QK, Q_LORA)
    wuq = jnp.pad(wuq, ((0, 0), (0, C_QK_PAD - C_QK), (0, 0))).reshape(C_HEADS * C_QK_PAD, Q_LORA).astype(BF16)
    wukv = mla_w_ukv[l].T.astype(BF16)
    bblk = jnp.concatenate([_block_diag(jnp.swapaxes(ssm_b_re[l], 1, 2)),
                            _block_diag(jnp.swapaxes(ssm_b_im[l], 1, 2))], axis=1).astype(BF16)
    cre = _block_diag(jnp.swapaxes(ssm_c_re[l], 1, 2)).astype(BF16)
    cim = _block_diag(jnp.swapaxes(ssm_c_im[l], 1, 2)).astype(BF16)
    lam = jnp.stack([ssm_a_re[l].reshape(-1), ssm_a_im[l].reshape(-1),
                     jnp.repeat(ssm_log_step[l], S5_STATE)]).astype(F32)
    lam = jnp.concatenate([lam, jnp.zeros((5, S5_CH), F32)], axis=0)
    dskip = ssm_d[l].reshape(1, S5_WIDTH).astype(F32)
    return dict(ng=norm_g[l][None, :].astype(F32), wn=wn, wt=wt, gcol=gcol, wuq=wuq, wukv=wukv,
                bblk=bblk, cre=cre, cim=cim, lam=lam, dskip=dskip, wglu=ssm_w_glu[l].astype(BF16),
                wo=w_out[l].astype(BF16))


def kernel(x, norm_g, w_in, attn_q_norm, attn_k_norm, mla_q_lora_norm, mla_kv_lora_norm, mla_w_uq, mla_w_ukv, mla_q_norm, mla_k_norm, ssm_a_re, ssm_a_im, ssm_b_re, ssm_b_im, ssm_c_re, ssm_c_im, ssm_d, ssm_log_step, ssm_w_glu, w_out):
    b, s, _ = x.shape
    ts = min(512, s)
    tq = min(512, s)
    s5_steps = 32
    rope = _rope_table(s)
    params = (norm_g, w_in, attn_q_norm, attn_k_norm, mla_q_lora_norm, mla_kv_lora_norm, mla_w_uq,
              mla_w_ukv, mla_q_norm, mla_k_norm, ssm_a_re, ssm_a_im, ssm_b_re, ssm_b_im, ssm_c_re,
              ssm_c_im, ssm_d, ssm_log_step, ssm_w_glu, w_out)
    for l in range(w_in.shape[0]):
        p = _layer_operands(l, *params)
        (g, u, qt, k, vt, iqt, ik, iwt, cqt, ck, cvt) = _prep_call(
            x, p["ng"], p["wn"], p["wt"], p["gcol"], rope, p["wuq"], p["wukv"], ts)
        ya = _dsa_call(iqt, iwt, ik, qt, k, vt, tq)
        yc = _mla_call(cqt, ck, cvt, tq)
        yb = _s5_call(u.reshape(s * b, S5_WIDTH), p["bblk"], p["cre"], p["cim"], p["lam"],
                      p["dskip"], p["wglu"], b, s5_steps)
        x = _out_call(x, g, ya, yb.reshape(s, b * S5_WIDTH), yc, p["wo"], ts)
    return x
```

```python
import functools

import numpy as np
import jax
import jax.numpy as jnp
from jax import lax
from jax.experimental import pallas as pl
from jax.experimental.pallas import tpu as pltpu

F32 = jnp.float32
BF16 = jnp.bfloat16
I32 = jnp.int32
I16 = jnp.int16

TILE_UNROLL = 2
IDX_ROWS = 128
LOG2E = float(np.log2(np.e))

CHUNK = 64
ROPE_THETA = 10000.0
EPS = 1e-6

A_HEADS = 6
A_HD = 64
A_WIDTH = A_HEADS * A_HD
IDX_HEADS = 8
IDX_DIM = 32
TOPK_MAX = 256

S5_GROUP = 16
S5_WIDTH = 256
S5_GROUPS = S5_WIDTH // S5_GROUP
S5_STATE = 64
S5_CH = S5_GROUPS * S5_STATE

C_HEADS = 6
C_NOPE = 64
C_ROPE = 32
C_VDIM = 64
C_QK = C_NOPE + C_ROPE
C_QK_PAD = 128
C_WIDTH = C_HEADS * C_VDIM
Q_LORA = 256
KV_LORA = 128

SPLIT_SIZES = (A_WIDTH, A_HD, A_HD, IDX_HEADS * IDX_DIM, IDX_DIM, IDX_HEADS, A_WIDTH,
               S5_WIDTH, S5_WIDTH, Q_LORA, KV_LORA, C_ROPE, C_WIDTH)

T_QA, T_KA, T_VA, T_IQ, T_IK, T_CQ, T_CKV, T_KPE, T_IW, T_END = (
    0, 384, 448, 512, 768, 800, 1056, 1184, 1216, 1232)
G_AQ, G_AK, G_QL, G_KVL, G_MQ, G_MK, G_END = 0, 64, 128, 384, 512, 608, 704
R_C64, R_S64, R_C32, R_S32, R_END = 0, 32, 64, 80, 96

INT_MIN = np.int32(-2 ** 31)
NEG_INF = float("-inf")

VMEM_LIMIT = 56 * 1024 * 1024


def _cparams(sem):
    return pltpu.CompilerParams(dimension_semantics=sem, vmem_limit_bytes=VMEM_LIMIT)


def _rms_rows(x, gcol):
    r = lax.rsqrt(jnp.mean(x * x, axis=0, keepdims=True) + EPS)
    return x * r * gcol


def _rope_rows(x, cos, sin):
    half = x.shape[0] // 2
    x1, x2 = x[:half], x[half:]
    return jnp.concatenate([x1 * cos - x2 * sin, x2 * cos + x1 * sin], axis=0)


def _rows_to_token_major(xt, width):
    d, t = xt.shape
    if d < 128:
        xt = jnp.concatenate([xt, jnp.zeros((128 - d, t), F32)], axis=0)
    return xt.T[:, :width]


def _prep_kernel(x_ref, ng_ref, wn_ref, wt_ref, gcol_ref, rope_ref, wuq_ref, wukv_ref,
                 g_out, u_out, qt_out, k_out, vt_out, iqt_out, ik_out, iwt_out,
                 cqt_out, ck_out, cvt_out):
    x = x_ref[0]
    ms = jnp.mean(x * x, axis=-1, keepdims=True)
    h = (x * lax.rsqrt(ms + EPS) * ng_ref[...]).astype(BF16)

    pn = jnp.dot(h, wn_ref[...], preferred_element_type=F32)
    g_out[0] = pn[:, :1024].astype(BF16)
    u_out[...] = pn[:, 1024:1280]

    def tdot(lo, hi):
        return lax.dot_general(wt_ref[lo:hi, :], h, (((1,), (1,)), ((), ())),
                               preferred_element_type=F32)

    cos64 = rope_ref[R_C64:R_S64, :]
    sin64 = rope_ref[R_S64:R_C32, :]
    cos32 = rope_ref[R_C32:R_S32, :]
    sin32 = rope_ref[R_S32:R_END, :]

    pa = tdot(T_QA, T_IQ)
    g_aq = gcol_ref[G_AQ:G_AK, :]
    for hd in range(A_HEADS):
        qh = _rope_rows(_rms_rows(pa[hd * A_HD:(hd + 1) * A_HD], g_aq), cos64, sin64)
        qt_out[0, hd * A_HD:(hd + 1) * A_HD, :] = (qh * (LOG2E * A_HD ** -0.5)).astype(BF16)
    kt = _rope_rows(_rms_rows(pa[T_KA:T_VA], gcol_ref[G_AK:G_QL, :]), cos64, sin64)
    k_out[0] = _rows_to_token_major(kt, A_HD).astype(BF16)
    vt_out[0] = pa[T_VA:T_IQ].astype(BF16)

    pb = tdot(T_IQ, T_CQ)
    for hd in range(IDX_HEADS):
        iqh = _rope_rows(pb[hd * IDX_DIM:(hd + 1) * IDX_DIM], cos32, sin32)
        iqt_out[0, hd * IDX_DIM:(hd + 1) * IDX_DIM, :] = iqh.astype(BF16)
    ikt = _rope_rows(pb[T_IK - T_IQ:T_CQ - T_IQ], cos32, sin32)
    ik_out[0] = _rows_to_token_major(ikt, IDX_DIM).astype(BF16)

    pc = tdot(T_CQ, T_END)
    iwt_out[0] = pc[T_IW - T_CQ:T_IW - T_CQ + IDX_HEADS] * ((IDX_HEADS ** -0.5) * (IDX_DIM ** -0.5))

    cqn = _rms_rows(pc[:Q_LORA], gcol_ref[G_QL:G_KVL, :]).astype(BF16)
    qt = jnp.dot(wuq_ref[...], cqn, preferred_element_type=F32)
    g_mq = gcol_ref[G_MQ:G_MK, :]
    zpad = jnp.zeros((C_QK_PAD - C_QK, x.shape[0]), F32)
    for hd in range(C_HEADS):
        qh = _rms_rows(qt[hd * C_QK_PAD:hd * C_QK_PAD + C_QK], g_mq)
        qh = jnp.concatenate([qh[:C_NOPE], _rope_rows(qh[C_NOPE:], cos32, sin32), zpad], axis=0)
        cqt_out[0, hd * C_QK_PAD:(hd + 1) * C_QK_PAD, :] = (qh * (LOG2E * C_QK ** -0.5)).astype(BF16)

    ckvn = _rms_rows(pc[T_CKV - T_CQ:T_KPE - T_CQ], gcol_ref[G_KVL:G_MQ, :]).astype(BF16)
    kvt = jnp.dot(wukv_ref[...], ckvn, preferred_element_type=F32)
    kpe = pc[T_KPE - T_CQ:T_IW - T_CQ]
    kpe_ss = jnp.sum(kpe * kpe, axis=0, keepdims=True)
    g_mk = gcol_ref[G_MK:G_END, :]
    for hd in range(C_HEADS):
        kn = kvt[hd * 128:hd * 128 + C_NOPE]
        ss = jnp.sum(kn * kn, axis=0, keepdims=True) + kpe_ss
        r = lax.rsqrt(ss * (1.0 / C_QK) + EPS)
        kh = jnp.concatenate([kn * r * g_mk[:C_NOPE],
                              _rope_rows(kpe * r * g_mk[C_NOPE:], cos32, sin32), zpad], axis=0)
        ck_out[0, hd] = kh.T.astype(BF16)
        cvt_out[0, hd * C_VDIM:(hd + 1) * C_VDIM, :] = kvt[hd * 128 + C_NOPE:(hd + 1) * 128].astype(BF16)


def _prep_call(x, ng, wn, wt, gcol, rope, wuq, wukv, ts):
    b, s, d = x.shape
    grid = (b, s // ts)
    tok = lambda w: pl.BlockSpec((1, ts, w), lambda bi, i: (bi, i, 0))
    feat = lambda r: pl.BlockSpec((1, r, ts), lambda bi, i: (bi, 0, i))
    full = lambda a: pl.BlockSpec(a.shape, lambda bi, i: (0,) * a.ndim)
    out_shape = (
        jax.ShapeDtypeStruct((b, s, 1024), BF16),
        jax.ShapeDtypeStruct((s, b * S5_WIDTH), F32),
        jax.ShapeDtypeStruct((b, A_WIDTH, s), BF16),
        jax.ShapeDtypeStruct((b, s, A_HD), BF16),
        jax.ShapeDtypeStruct((b, A_HD, s), BF16),
        jax.ShapeDtypeStruct((b, IDX_HEADS * IDX_DIM, s), BF16),
        jax.ShapeDtypeStruct((b, s, IDX_DIM), BF16),
        jax.ShapeDtypeStruct((b, IDX_HEADS, s), F32),
        jax.ShapeDtypeStruct((b, C_HEADS * C_QK_PAD, s), BF16),
        jax.ShapeDtypeStruct((b, C_HEADS, s, C_QK_PAD), BF16),
        jax.ShapeDtypeStruct((b, C_WIDTH, s), BF16),
    )
    out_specs = (
        tok(1024),
        pl.BlockSpec((ts, S5_WIDTH), lambda bi, i: (i, bi)),
        feat(A_WIDTH), tok(A_HD), feat(A_HD),
        feat(IDX_HEADS * IDX_DIM), tok(IDX_DIM), feat(IDX_HEADS),
        feat(C_HEADS * C_QK_PAD),
        pl.BlockSpec((1, C_HEADS, ts, C_QK_PAD), lambda bi, i: (bi, 0, i, 0)),
        feat(C_WIDTH),
    )
    in_specs = [tok(d), full(ng), full(wn), full(wt), full(gcol),
                pl.BlockSpec((R_END, ts), lambda bi, i: (0, i)), full(wuq), full(wukv)]
    return pl.pallas_call(
        _prep_kernel, grid=grid, in_specs=in_specs, out_specs=out_specs, out_shape=out_shape,
        compiler_params=_cparams(("parallel", "parallel")), name="prep",
    )(x, ng, wn, wt, gcol, rope, wuq, wukv)


def _chunk_causal_bias(tk, tq):
    kc = lax.broadcasted_iota(I32, (tk, tq), 0) // CHUNK
    qc = lax.broadcasted_iota(I32, (tk, tq), 1) // CHUNK
    return jnp.where(kc <= qc, 0.0, NEG_INF).astype(F32)


def _col_reduce8(x, op):
    tk, tq = x.shape
    x = x.reshape(tk // 8, 8, tq)
    return op(x, axis=0)


def _for_tiles(nt, tk, body, carry):
    nfull = nt // TILE_UNROLL

    def group(i, c):
        for k in range(TILE_UNROLL):
            c = body(pl.multiple_of((i * TILE_UNROLL + k) * tk, tk), c)
        return c

    carry = lax.fori_loop(0, nfull, group, carry)
    return lax.fori_loop(nfull * TILE_UNROLL, nt, lambda kt, c: body(pl.multiple_of(kt * tk, tk), c), carry)


def _attention_heads(nheads, nt, tk, tq, dv, k_tile, q_t, vt_tile, s_refs, write_out,
                     bias_tile=None, diag_bias=None):
    def score_pass(h, r0, m8, bias):
        s = jnp.dot(k_tile(h, r0), q_t(h), preferred_element_type=F32)
        if bias is not None:
            s = s + bias
        s_refs[h % 2][pl.ds(r0, tk), :] = s
        return jnp.maximum(m8, _col_reduce8(s, jnp.max))

    def value_pass(h, r0, m, l8, acc):
        p = jnp.exp2(s_refs[h % 2][pl.ds(r0, tk), :] - m)
        l8 = l8 + _col_reduce8(p, jnp.sum)
        acc = acc + jnp.dot(vt_tile(h, r0), p.astype(BF16), preferred_element_type=F32)
        return l8, acc

    def tile_step(h_score, h_value, m, r0, carry, bias):
        m8, l8, acc = carry
        if h_score is not None:
            m8 = score_pass(h_score, r0, m8, bias)
        if h_value is not None:
            l8, acc = value_pass(h_value, r0, m, l8, acc)
        return m8, l8, acc

    m = None
    for h in range(nheads + 1):
        h_score = h if h < nheads else None
        h_value = h - 1 if h > 0 else None
        carry = (jnp.full((8, tq), NEG_INF, F32), jnp.zeros((8, tq), F32), jnp.zeros((dv, tq), F32))
        carry = _for_tiles(
            nt, tk,
            lambda r0, c: tile_step(h_score, h_value, m, r0, c,
                                    None if (bias_tile is None or h_score is None) else bias_tile(r0)),
            carry)
        if diag_bias is not None:
            carry = tile_step(h_score, h_value, m, pl.multiple_of(nt * tk, tk), carry, diag_bias)
        m8, l8, acc = carry
        if h_value is not None:
            write_out(h_value, acc / jnp.sum(l8, axis=0, keepdims=True))
        m = jnp.max(m8, axis=0, keepdims=True)


def _dsa_kernel(iqt_ref, iwt_ref, ik_ref, qt_ref, k_ref, vt_ref, ya_out,
                keys_ref, hi_ref, lo_ref, s0_ref, s1_ref, ot_ref, thr_ref, jcut_ref, trim_ref, *, tq, seq, ktop):
    tk = tq
    j = pl.program_id(1)
    nt = j + 1

    def put_keys(r0, diagonal):
        for sb in range(tk // IDX_ROWS):
            rs = pl.multiple_of(r0 + sb * IDX_ROWS, IDX_ROWS)
            ikt = ik_ref[0, pl.ds(rs, IDX_ROWS), :]
            acc = jnp.zeros((IDX_ROWS, tq), F32)
            for hd in range(IDX_HEADS):
                sh = jnp.dot(ikt, iqt_ref[0, hd * IDX_DIM:(hd + 1) * IDX_DIM, :],
                             preferred_element_type=F32)
                acc = acc + jnp.maximum(sh, 0.0) * iwt_ref[0, hd:hd + 1, :]
            bits = lax.bitcast_convert_type(acc, I32)
            key = bits ^ ((bits >> 31) & np.int32(0x7FFFFFFF))
            if diagonal:
                kc = (sb * IDX_ROWS + lax.broadcasted_iota(I32, (IDX_ROWS, tq), 0)) // CHUNK
                qc = lax.broadcasted_iota(I32, (IDX_ROWS, tq), 1) // CHUNK
                key = jnp.where(kc <= qc, key, INT_MIN)
            keys_ref[pl.ds(rs, IDX_ROWS), :] = key
            hi_ref[pl.ds(rs, IDX_ROWS), :] = (key >> 16).astype(I16)
            lo_ref[pl.ds(rs, IDX_ROWS), :] = ((key & np.int32(0xFFFF)) - 32768).astype(I16)

    def idx_tile(kt, c):
        put_keys(pl.multiple_of(kt * tk, tk), False)
        return c

    lax.fori_loop(0, j, idx_tile, 0)
    put_keys(pl.multiple_of(j * tk, tk), True)

    def count(pred):
        def body(r0, c8):
            kk = keys_ref[pl.ds(r0, tk), :]
            row = r0 + lax.broadcasted_iota(I32, (tk, tq), 0)
            return c8 + _col_reduce8(pred(kk, row).astype(I32), jnp.sum)
        c8 = _for_tiles(nt, tk, body, jnp.zeros((8, tq), I32))
        return jnp.sum(c8, axis=0, keepdims=True)

    def count16(ref, pred):
        def body(r0, c16):
            m = pred(ref[pl.ds(r0, tk), :]).astype(I16)
            for q in range(tk // 16):
                c16 = c16 + m[q * 16:(q + 1) * 16]
            return c16
        c16 = _for_tiles(nt, tk, body, jnp.zeros((16, tq), I16))
        return jnp.sum(c16.astype(I32), axis=0, keepdims=True)

    def radix16(ref, kth):
        def bit_step(i, u):
            cand = u | lax.shift_left(np.int32(1), 15 - i)
            cand16 = (cand - 32768).astype(I16)
            cnt = count16(ref, lambda v: v >= cand16)
            return jnp.where(cnt >= kth, cand, u)
        return lax.fori_loop(0, 16, bit_step, jnp.zeros((1, tq), I32))

    keep_all = INT_MIN + np.int32(1)

    @pl.when(nt * tk <= ktop)
    def _dense():
        thr_ref[...] = jnp.full((1, tq), keep_all, I32)
        trim_ref[0] = 0

    @pl.when(nt * tk > ktop)
    def _topk():
        hi_thr = radix16(hi_ref, ktop) - 32768
        hi_thr16 = hi_thr.astype(I16)
        n_above = count16(hi_ref, lambda v: v > hi_thr16)

        def mask_lo(r0, c):
            same = hi_ref[pl.ds(r0, tk), :] == hi_thr16
            lo_ref[pl.ds(r0, tk), :] = jnp.where(same, lo_ref[pl.ds(r0, tk), :], np.int16(-32768))
            return c

        _for_tiles(nt, tk, mask_lo, 0)
        lo_thr = radix16(lo_ref, ktop - n_above)
        qpos = j * tq + lax.broadcasted_iota(I32, (1, tq), 1)
        all_kept = (qpos // CHUNK + 1) * CHUNK <= ktop
        thr = jnp.where(all_kept, keep_all, lax.shift_left(hi_thr, 16) | lo_thr)
        thr_ref[...] = thr
        trim_ref[0] = 0
        n_ge = count(lambda kk, row: kk >= thr)

        @pl.when(jnp.max(n_ge) > ktop)
        def _ties():
            need = ktop - count(lambda kk, row: kk > thr)
            nbits = int(seq - 1).bit_length()

            def row_step(i, jp):
                cand = jp | lax.shift_left(np.int32(1), nbits - 1 - i)
                c = count(lambda kk, row: (kk == thr) & (row < cand))
                return jnp.where(c < need, cand, jp)

            jcut_ref[...] = lax.fori_loop(0, nbits, row_step, jnp.zeros((1, tq), I32)) + 1
            trim_ref[0] = 1

    thr = thr_ref[...]

    def put_mask(r0, sel):
        keys_ref[pl.ds(r0, tk), :] = lax.bitcast_convert_type(
            jnp.where(sel, 0.0, NEG_INF).astype(F32), I32)

    @pl.when(trim_ref[0] == 0)
    def _mask_plain():
        def fill(r0, c):
            put_mask(r0, keys_ref[pl.ds(r0, tk), :] >= thr)
            return c
        _for_tiles(nt, tk, fill, 0)

    @pl.when(trim_ref[0] != 0)
    def _mask_trimmed():
        jcut = jcut_ref[...]

        def fill(r0, c):
            kk = keys_ref[pl.ds(r0, tk), :]
            row = r0 + lax.broadcasted_iota(I32, (tk, tq), 0)
            put_mask(r0, (kk > thr) | ((kk == thr) & (row < jcut)))
            return c
        _for_tiles(nt, tk, fill, 0)

    def write_out(hd, o):
        ot_ref[hd * A_HD:(hd + 1) * A_HD, :] = o

    _attention_heads(
        A_HEADS, nt, tk, tq, A_HD,
        lambda hd, r0: k_ref[0, pl.ds(r0, tk), :],
        lambda hd: qt_ref[0, hd * A_HD:(hd + 1) * A_HD, :],
        lambda hd, r0: vt_ref[0, :, pl.ds(r0, tk)],
        (s0_ref, s1_ref), write_out,
        bias_tile=lambda r0: lax.bitcast_convert_type(keys_ref[pl.ds(r0, tk), :], F32))
    ya_out[0] = ot_ref[...].T.astype(BF16)


def _dsa_call(iqt, iwt, ik, qt, k, vt, tq):
    b, _, s = qt.shape
    ktop = min(TOPK_MAX, s // 4)
    assert s % tq == 0 and tq % CHUNK == 0 and ktop % CHUNK == 0
    grid = (b, s // tq)
    qblk = lambda r: pl.BlockSpec((1, r, tq), lambda bi, j: (bi, 0, j))
    in_specs = [
        qblk(IDX_HEADS * IDX_DIM), qblk(IDX_HEADS),
        pl.BlockSpec((1, s, IDX_DIM), lambda bi, j: (bi, 0, 0)),
        qblk(A_WIDTH),
        pl.BlockSpec((1, s, A_HD), lambda bi, j: (bi, 0, 0)),
        pl.BlockSpec((1, A_HD, s), lambda bi, j: (bi, 0, 0)),
    ]
    return pl.pallas_call(
        functools.partial(_dsa_kernel, tq=tq, seq=s, ktop=ktop),
        grid=grid, in_specs=in_specs,
        out_specs=pl.BlockSpec((1, tq, A_WIDTH), lambda bi, j: (bi, j, 0)),
        out_shape=jax.ShapeDtypeStruct((b, s, A_WIDTH), BF16),
        scratch_shapes=[pltpu.VMEM((s, tq), I32), pltpu.VMEM((s, tq), I16), pltpu.VMEM((s, tq), I16),
                        pltpu.VMEM((s, tq), F32), pltpu.VMEM((s, tq), F32),
                        pltpu.VMEM((A_WIDTH, tq), F32), pltpu.VMEM((1, tq), I32), pltpu.VMEM((1, tq), I32),
                        pltpu.SMEM((1,), I32)],
        compiler_params=_cparams(("parallel", "arbitrary")), name="dsa",
    )(iqt, iwt, ik, qt, k, vt)


def _mla_kernel(cqt_ref, ck_ref, cvt_ref, yc_out, s0_ref, s1_ref, ot_ref, *, tq):
    tk = tq
    j = pl.program_id(1)

    def write_out(hd, o):
        ot_ref[hd * C_VDIM:(hd + 1) * C_VDIM, :] = o

    _attention_heads(
        C_HEADS, j, tk, tq, C_VDIM,
        lambda hd, r0: ck_ref[0, hd, pl.ds(r0, tk), :],
        lambda hd: cqt_ref[0, hd * C_QK_PAD:(hd + 1) * C_QK_PAD, :],
        lambda hd, r0: cvt_ref[0, hd * C_VDIM:(hd + 1) * C_VDIM, pl.ds(r0, tk)],
        (s0_ref, s1_ref), write_out, diag_bias=_chunk_causal_bias(tk, tq))
    yc_out[0] = ot_ref[...].T.astype(BF16)


def _mla_call(cqt, ck, cvt, tq):
    b, _, s = cqt.shape
    assert s % tq == 0 and tq % CHUNK == 0
    grid = (b, s // tq)
    in_specs = [
        pl.BlockSpec((1, C_HEADS * C_QK_PAD, tq), lambda bi, j: (bi, 0, j)),
        pl.BlockSpec((1, C_HEADS, s, C_QK_PAD), lambda bi, j: (bi, 0, 0, 0)),
        pl.BlockSpec((1, C_WIDTH, s), lambda bi, j: (bi, 0, 0)),
    ]
    return pl.pallas_call(
        functools.partial(_mla_kernel, tq=tq),
        grid=grid, in_specs=in_specs,
        out_specs=pl.BlockSpec((1, tq, C_WIDTH), lambda bi, j: (bi, j, 0)),
        out_shape=jax.ShapeDtypeStruct((b, s, C_WIDTH), BF16),
        scratch_shapes=[pltpu.VMEM((s, tq), F32), pltpu.VMEM((s, tq), F32), pltpu.VMEM((C_WIDTH, tq), F32)],
        compiler_params=_cparams(("parallel", "arbitrary")), name="mla",
    )(cqt, ck, cvt)


def _s5_kernel(u_ref, bblk_ref, cre_ref, cim_ref, lam_ref, dskip_ref, wglu_ref, yb_out,
               state_ref, bb_ref, xs_ref, *, nb, tsteps):
    @pl.when(pl.program_id(0) == 0)
    def _init():
        state_ref[...] = jnp.zeros_like(state_ref)

    ar, ai, step = lam_ref[0:1, :], lam_ref[1:2, :], jnp.exp(lam_ref[2:3, :])
    mag = jnp.exp(ar * step)
    abar_re = mag * jnp.cos(ai * step)
    abar_im = mag * jnp.sin(ai * step)
    den = ar * ar + ai * ai
    nr = abar_re - 1.0
    f_re = (nr * ar + abar_im * ai) / den
    f_im = (abar_im * ar - nr * ai) / den

    u = u_ref[...]
    bu = jnp.dot(u.astype(BF16), bblk_ref[...], preferred_element_type=F32)
    bu_re, bu_im = bu[:, :S5_CH], bu[:, S5_CH:]
    bb_ref[:, :S5_CH] = f_re * bu_re - f_im * bu_im
    bb_ref[:, S5_CH:] = f_re * bu_im + f_im * bu_re

    a_re = jnp.broadcast_to(abar_re, (nb, S5_CH))
    a_im = jnp.broadcast_to(abar_im, (nb, S5_CH))

    def scan_step(t, carry):
        xr, xi = carry
        r0 = pl.multiple_of(t * nb, nb)
        nxr = a_re * xr - a_im * xi + bb_ref[pl.ds(r0, nb), :S5_CH]
        nxi = a_re * xi + a_im * xr + bb_ref[pl.ds(r0, nb), S5_CH:]
        xs_ref[pl.ds(r0, nb), :S5_CH] = nxr
        xs_ref[pl.ds(r0, nb), S5_CH:] = nxi
        return nxr, nxi

    xr, xi = lax.fori_loop(0, tsteps, scan_step, (state_ref[:, :S5_CH], state_ref[:, S5_CH:]))
    state_ref[:, :S5_CH] = xr
    state_ref[:, S5_CH:] = xi

    y = (jnp.dot(xs_ref[:, :S5_CH].astype(BF16), cre_ref[...], preferred_element_type=F32)
         - jnp.dot(xs_ref[:, S5_CH:].astype(BF16), cim_ref[...], preferred_element_type=F32)
         + dskip_ref[...] * u)
    g = 0.5 * y * (1.0 + jnp.tanh(np.float32(np.sqrt(2.0 / np.pi)) * (y + 0.044715 * (y * y * y))))
    gate = jax.nn.sigmoid(jnp.dot(g.astype(BF16), wglu_ref[...], preferred_element_type=F32))
    yb_out[...] = (g * gate).astype(BF16)


def _s5_call(u2d, bblk, cre, cim, lam, dskip, wglu, nb, tsteps):
    rows = u2d.shape[0]
    blk = nb * tsteps
    assert rows % blk == 0 and nb % 8 == 0
    full = lambda a: pl.BlockSpec(a.shape, lambda i: (0,) * a.ndim)
    return pl.pallas_call(
        functools.partial(_s5_kernel, nb=nb, tsteps=tsteps),
        grid=(rows // blk,),
        in_specs=[pl.BlockSpec((blk, S5_WIDTH), lambda i: (i, 0)),
                  full(bblk), full(cre), full(cim), full(lam), full(dskip), full(wglu)],
        out_specs=pl.BlockSpec((blk, S5_WIDTH), lambda i: (i, 0)),
        out_shape=jax.ShapeDtypeStruct((rows, S5_WIDTH), BF16),
        scratch_shapes=[pltpu.VMEM((nb, 2 * S5_CH), F32), pltpu.VMEM((blk, 2 * S5_CH), F32),
                        pltpu.VMEM((blk, 2 * S5_CH), F32)],
        compiler_params=_cparams(("arbitrary",)), name="s5",
    )(u2d, bblk, cre, cim, lam, dskip, wglu)


def _out_kernel(x_ref, g_ref, ya_ref, yb_ref, yc_ref, wo_ref, o_ref):
    g = g_ref[0].astype(F32)
    sg = g * jax.nn.sigmoid(g)
    ma = (ya_ref[0].astype(F32) * sg[:, :A_WIDTH]).astype(BF16)
    mb = (yb_ref[...].astype(F32) * sg[:, A_WIDTH:A_WIDTH + S5_WIDTH]).astype(BF16)
    mc = (yc_ref[0].astype(F32) * sg[:, A_WIDTH + S5_WIDTH:]).astype(BF16)
    o_ref[0] = (x_ref[0]
                + jnp.dot(ma, wo_ref[:A_WIDTH, :], preferred_element_type=F32)
                + jnp.dot(mb, wo_ref[A_WIDTH:A_WIDTH + S5_WIDTH, :], preferred_element_type=F32)
                + jnp.dot(mc, wo_ref[A_WIDTH + S5_WIDTH:, :], preferred_element_type=F32))


def _out_call(x, g, ya, yb, yc, wo, ts):
    b, s, d = x.shape
    tok = lambda w: pl.BlockSpec((1, ts, w), lambda bi, i: (bi, i, 0))
    return pl.pallas_call(
        _out_kernel, grid=(b, s // ts),
        in_specs=[tok(d), tok(1024), tok(A_WIDTH),
                  pl.BlockSpec((ts, S5_WIDTH), lambda bi, i: (i, bi)), tok(C_WIDTH),
                  pl.BlockSpec(wo.shape, lambda bi, i: (0, 0))],
        out_specs=tok(d), out_shape=jax.ShapeDtypeStruct((b, s, d), F32),
        compiler_params=_cparams(("parallel", "parallel")), name="outproj",
    )(x, g, ya, yb, yc, wo)


def _rope_table(s):
    pos = jnp.arange(s, dtype=jnp.int32).astype(F32)

    def cs(d):
        half = d // 2
        inv = ROPE_THETA ** (-jnp.arange(half, dtype=F32) * 2.0 / d)
        ang = pos[:, None] * inv[None, :]
        return jnp.cos(ang).T, jnp.sin(ang).T

    c64, s64 = cs(A_HD)
    c32, s32 = cs(IDX_DIM)
    return jnp.concatenate([c64, s64, c32, s32], axis=0)


def _block_diag(w):
    g, r, c = w.shape
    eye = jnp.eye(g, dtype=w.dtype)
    return (w[:, :, None, :] * eye[:, None, :, None]).reshape(g * r, g * c)


def _layer_operands(l, norm_g, w_in, attn_q_norm, attn_k_norm, mla_q_lora_norm, mla_kv_lora_norm,
                    mla_w_uq, mla_w_ukv, mla_q_norm, mla_k_norm, ssm_a_re, ssm_a_im, ssm_b_re,
                    ssm_b_im, ssm_c_re, ssm_c_im, ssm_d, ssm_log_step, ssm_w_glu, w_out):
    pts = [int(v) for v in np.cumsum(SPLIT_SIZES)[:-1]]
    (qa, ka, va, iq, ik, iw, ga, u, gb, cq, ckv, kpe, gc) = jnp.split(w_in[l], pts, axis=-1)
    wn = jnp.concatenate([ga, gb, gc, u], axis=1).astype(BF16)
    pad = jnp.zeros((w_in.shape[1], T_END - T_IW - IDX_HEADS), w_in.dtype)
    wt = jnp.concatenate([qa, ka, va, iq, ik, cq, ckv, kpe, iw, pad], axis=1).T.astype(BF16)
    gcol = jnp.concatenate([attn_q_norm[l], attn_k_norm[l], mla_q_lora_norm[l], mla_kv_lora_norm[l],
                            mla_q_norm[l], mla_k_norm[l]]).astype(F32)[:, None]
    wuq = mla_w_uq[l].T.reshape(C_HEADS, C_QK, Q_LORA)
    wuq = jnp.pad(wuq, ((0, 0), (0, C_QK_PAD - C_QK), (0, 0))).reshape(C_HEADS * C_QK_PAD, Q_LORA).astype(BF16)
    wukv = mla_w_ukv[l].T.astype(BF16)
    bblk = jnp.concatenate([_block_diag(jnp.swapaxes(ssm_b_re[l], 1, 2)),
                            _block_diag(jnp.swapaxes(ssm_b_im[l], 1, 2))], axis=1).astype(BF16)
    cre = _block_diag(jnp.swapaxes(ssm_c_re[l], 1, 2)).astype(BF16)
    cim = _block_diag(jnp.swapaxes(ssm_c_im[l], 1, 2)).astype(BF16)
    lam = jnp.stack([ssm_a_re[l].reshape(-1), ssm_a_im[l].reshape(-1),
                     jnp.repeat(ssm_log_step[l], S5_STATE)]).astype(F32)
    lam = jnp.concatenate([lam, jnp.zeros((5, S5_CH), F32)], axis=0)
    dskip = ssm_d[l].reshape(1, S5_WIDTH).astype(F32)
    return dict(ng=norm_g[l][None, :].astype(F32), wn=wn, wt=wt, gcol=gcol, wuq=wuq, wukv=wukv,
                bblk=bblk, cre=cre, cim=cim, lam=lam, dskip=dskip, wglu=ssm_w_glu[l].astype(BF16),
                wo=w_out[l].astype(BF16))


def kernel(x, norm_g, w_in, attn_q_norm, attn_k_norm, mla_q_lora_norm, mla_kv_lora_norm, mla_w_uq, mla_w_ukv, mla_q_norm, mla_k_norm, ssm_a_re, ssm_a_im, ssm_b_re, ssm_b_im, ssm_c_re, ssm_c_im, ssm_d, ssm_log_step, ssm_w_glu, w_out):
    b, s, _ = x.shape
    ts = min(512, s)
    tq = min(512, s)
    s5_steps = 32
    rope = _rope_table(s)
    params = (norm_g, w_in, attn_q_norm, attn_k_norm, mla_q_lora_norm, mla_kv_lora_norm, mla_w_uq,
              mla_w_ukv, mla_q_norm, mla_k_norm, ssm_a_re, ssm_a_im, ssm_b_re, ssm_b_im, ssm_c_re,
              ssm_c_im, ssm_d, ssm_log_step, ssm_w_glu, w_out)
    for l in range(w_in.shape[0]):
        p = _layer_operands(l, *params)
        (g, u, qt, k, vt, iqt, ik, iwt, cqt, ck, cvt) = _prep_call(
            x, p["ng"], p["wn"], p["wt"], p["gcol"], rope, p["wuq"], p["wukv"], ts)
        ya = _dsa_call(iqt, iwt, ik, qt, k, vt, tq)
        yc = _mla_call(cqt, ck, cvt, tq)
        yb = _s5_call(u.reshape(s * b, S5_WIDTH), p["bblk"], p["cre"], p["cim"], p["lam"],
                      p["dskip"], p["wglu"], b, s5_steps)
        x = _out_call(x, g, ya, yb.reshape(s, b * S5_WIDTH), yc, p["wo"], ts)
    return x
```

```python
import functools

import numpy as np
import jax
import jax.numpy as jnp
from jax import lax
from jax.experimental import pallas as pl
from jax.experimental.pallas import tpu as pltpu

F32 = jnp.float32
BF16 = jnp.bfloat16
I32 = jnp.int32

TILE_UNROLL = 2
IDX_ROWS = 128
PLANE_ROWS = 256
LOG2E = float(np.log2(np.e))

CHUNK = 64
ROPE_THETA = 10000.0
EPS = 1e-6

A_HEADS = 6
A_HD = 64
A_WIDTH = A_HEADS * A_HD
IDX_HEADS = 8
IDX_DIM = 32
TOPK_MAX = 256

S5_GROUP = 16
S5_WIDTH = 256
S5_GROUPS = S5_WIDTH // S5_GROUP
S5_STATE = 64
S5_CH = S5_GROUPS * S5_STATE

C_HEADS = 6
C_NOPE = 64
C_ROPE = 32
C_VDIM = 64
C_QK = C_NOPE + C_ROPE
C_QK_PAD = 128
C_WIDTH = C_HEADS * C_VDIM
Q_LORA = 256
KV_LORA = 128

SPLIT_SIZES = (A_WIDTH, A_HD, A_HD, IDX_HEADS * IDX_DIM, IDX_DIM, IDX_HEADS, A_WIDTH,
               S5_WIDTH, S5_WIDTH, Q_LORA, KV_LORA, C_ROPE, C_WIDTH)

T_QA, T_KA, T_VA, T_IQ, T_IK, T_CQ, T_CKV, T_KPE, T_IW, T_END = (
    0, 384, 448, 512, 768, 800, 1056, 1184, 1216, 1232)
G_AQ, G_AK, G_QL, G_KVL, G_MQ, G_MK, G_END = 0, 64, 128, 384, 512, 608, 704
R_C64, R_S64, R_C32, R_S32, R_END = 0, 32, 64, 80, 96

INT_MIN = np.int32(-2 ** 31)
NEG_INF = float("-inf")

VMEM_LIMIT = 56 * 1024 * 1024


def _cparams(sem):
    return pltpu.CompilerParams(dimension_semantics=sem, vmem_limit_bytes=VMEM_LIMIT)


def _rms_rows(x, gcol):
    r = lax.rsqrt(jnp.mean(x * x, axis=0, keepdims=True) + EPS)
    return x * r * gcol


def _rope_rows(x, cos, sin):
    half = x.shape[0] // 2
    x1, x2 = x[:half], x[half:]
    return jnp.concatenate([x1 * cos - x2 * sin, x2 * cos + x1 * sin], axis=0)


def _rows_to_token_major(xt, width):
    d, t = xt.shape
    if d < 128:
        xt = jnp.concatenate([xt, jnp.zeros((128 - d, t), F32)], axis=0)
    return xt.T[:, :width]


def _prep_kernel(x_ref, ng_ref, wn_ref, wt_ref, gcol_ref, rope_ref, wuq_ref, wukv_ref,
                 g_out, u_out, qt_out, k_out, vt_out, iqt_out, ik_out, iwt_out,
                 cqt_out, ck_out, cvt_out):
    x = x_ref[0]
    ms = jnp.mean(x * x, axis=-1, keepdims=True)
    h = (x * lax.rsqrt(ms + EPS) * ng_ref[...]).astype(BF16)

    pn = jnp.dot(h, wn_ref[...], preferred_element_type=F32)
    g_out[0] = pn[:, :1024].astype(BF16)
    u_out[...] = pn[:, 1024:1280]

    def tdot(lo, hi):
        return lax.dot_general(wt_ref[lo:hi, :], h, (((1,), (1,)), ((), ())),
                               preferred_element_type=F32)

    cos64 = rope_ref[R_C64:R_S64, :]
    sin64 = rope_ref[R_S64:R_C32, :]
    cos32 = rope_ref[R_C32:R_S32, :]
    sin32 = rope_ref[R_S32:R_END, :]

    pa = tdot(T_QA, T_IQ)
    g_aq = gcol_ref[G_AQ:G_AK, :]
    for hd in range(A_HEADS):
        qh = _rope_rows(_rms_rows(pa[hd * A_HD:(hd + 1) * A_HD], g_aq), cos64, sin64)
        qt_out[0, hd * A_HD:(hd + 1) * A_HD, :] = (qh * (LOG2E * A_HD ** -0.5)).astype(BF16)
    kt = _rope_rows(_rms_rows(pa[T_KA:T_VA], gcol_ref[G_AK:G_QL, :]), cos64, sin64)
    k_out[0] = _rows_to_token_major(kt, A_HD).astype(BF16)
    vt_out[0] = pa[T_VA:T_IQ].astype(BF16)

    pb = tdot(T_IQ, T_CQ)
    for hd in range(IDX_HEADS):
        iqh = _rope_rows(pb[hd * IDX_DIM:(hd + 1) * IDX_DIM], cos32, sin32)
        iqt_out[0, hd * IDX_DIM:(hd + 1) * IDX_DIM, :] = iqh.astype(BF16)
    ikt = _rope_rows(pb[T_IK - T_IQ:T_CQ - T_IQ], cos32, sin32)
    ik_out[0] = _rows_to_token_major(ikt, IDX_DIM).astype(BF16)

    pc = tdot(T_CQ, T_END)
    iwt_out[0] = pc[T_IW - T_CQ:T_IW - T_CQ + IDX_HEADS] * ((IDX_HEADS ** -0.5) * (IDX_DIM ** -0.5))

    cqn = _rms_rows(pc[:Q_LORA], gcol_ref[G_QL:G_KVL, :]).astype(BF16)
    qt = jnp.dot(wuq_ref[...], cqn, preferred_element_type=F32)
    g_mq = gcol_ref[G_MQ:G_MK, :]
    zpad = jnp.zeros((C_QK_PAD - C_QK, x.shape[0]), F32)
    for hd in range(C_HEADS):
        qh = _rms_rows(qt[hd * C_QK_PAD:hd * C_QK_PAD + C_QK], g_mq)
        qh = jnp.concatenate([qh[:C_NOPE], _rope_rows(qh[C_NOPE:], cos32, sin32), zpad], axis=0)
        cqt_out[0, hd * C_QK_PAD:(hd + 1) * C_QK_PAD, :] = (qh * (LOG2E * C_QK ** -0.5)).astype(BF16)

    ckvn = _rms_rows(pc[T_CKV - T_CQ:T_KPE - T_CQ], gcol_ref[G_KVL:G_MQ, :]).astype(BF16)
    kvt = jnp.dot(wukv_ref[...], ckvn, preferred_element_type=F32)
    kpe = pc[T_KPE - T_CQ:T_IW - T_CQ]
    kpe_ss = jnp.sum(kpe * kpe, axis=0, keepdims=True)
    g_mk = gcol_ref[G_MK:G_END, :]
    for hd in range(C_HEADS):
        kn = kvt[hd * 128:hd * 128 + C_NOPE]
        ss = jnp.sum(kn * kn, axis=0, keepdims=True) + kpe_ss
        r = lax.rsqrt(ss * (1.0 / C_QK) + EPS)
        kh = jnp.concatenate([kn * r * g_mk[:C_NOPE],
                              _rope_rows(kpe * r * g_mk[C_NOPE:], cos32, sin32), zpad], axis=0)
        ck_out[0, hd] = kh.T.astype(BF16)
        cvt_out[0, hd * C_VDIM:(hd + 1) * C_VDIM, :] = kvt[hd * 128 + C_NOPE:(hd + 1) * 128].astype(BF16)


def _prep_call(x, ng, wn, wt, gcol, rope, wuq, wukv, ts):
    b, s, d = x.shape
    grid = (b, s // ts)
    tok = lambda w: pl.BlockSpec((1, ts, w), lambda bi, i: (bi, i, 0))
    feat = lambda r: pl.BlockSpec((1, r, ts), lambda bi, i: (bi, 0, i))
    full = lambda a: pl.BlockSpec(a.shape, lambda bi, i: (0,) * a.ndim)
    out_shape = (
        jax.ShapeDtypeStruct((b, s, 1024), BF16),
        jax.ShapeDtypeStruct((s, b * S5_WIDTH), F32),
        jax.ShapeDtypeStruct((b, A_WIDTH, s), BF16),
        jax.ShapeDtypeStruct((b, s, A_HD), BF16),
        jax.ShapeDtypeStruct((b, A_HD, s), BF16),
        jax.ShapeDtypeStruct((b, IDX_HEADS * IDX_DIM, s), BF16),
        jax.ShapeDtypeStruct((b, s, IDX_DIM), BF16),
        jax.ShapeDtypeStruct((b, IDX_HEADS, s), F32),
        jax.ShapeDtypeStruct((b, C_HEADS * C_QK_PAD, s), BF16),
        jax.ShapeDtypeStruct((b, C_HEADS, s, C_QK_PAD), BF16),
        jax.ShapeDtypeStruct((b, C_WIDTH, s), BF16),
    )
    out_specs = (
        tok(1024),
        pl.BlockSpec((ts, S5_WIDTH), lambda bi, i: (i, bi)),
        feat(A_WIDTH), tok(A_HD), feat(A_HD),
        feat(IDX_HEADS * IDX_DIM), tok(IDX_DIM), feat(IDX_HEADS),
        feat(C_HEADS * C_QK_PAD),
        pl.BlockSpec((1, C_HEADS, ts, C_QK_PAD), lambda bi, i: (bi, 0, i, 0)),
        feat(C_WIDTH),
    )
    in_specs = [tok(d), full(ng), full(wn), full(wt), full(gcol),
                pl.BlockSpec((R_END, ts), lambda bi, i: (0, i)), full(wuq), full(wukv)]
    return pl.pallas_call(
        _prep_kernel, grid=grid, in_specs=in_specs, out_specs=out_specs, out_shape=out_shape,
        compiler_params=_cparams(("parallel", "parallel")), name="prep",
    )(x, ng, wn, wt, gcol, rope, wuq, wukv)


def _chunk_causal_bias(tk, tq):
    kc = lax.broadcasted_iota(I32, (tk, tq), 0) // CHUNK
    qc = lax.broadcasted_iota(I32, (tk, tq), 1) // CHUNK
    return jnp.where(kc <= qc, 0.0, NEG_INF).astype(F32)


def _col_reduce8(x, op):
    tk, tq = x.shape
    x = x.reshape(tk // 8, 8, tq)
    return op(x, axis=0)


def _for_tiles(nt, tk, body, carry):
    nfull = nt // TILE_UNROLL

    def group(i, c):
        for k in range(TILE_UNROLL):
            c = body(pl.multiple_of((i * TILE_UNROLL + k) * tk, tk), c)
        return c

    carry = lax.fori_loop(0, nfull, group, carry)
    return lax.fori_loop(nfull * TILE_UNROLL, nt, lambda kt, c: body(pl.multiple_of(kt * tk, tk), c), carry)


def _attention_heads(nheads, nt, tk, tq, dv, k_tile, q_t, vt_tile, s_refs, write_out,
                     bias_tile=None, diag_bias=None):
    def score_pass(h, r0, m8, bias):
        s = jnp.dot(k_tile(h, r0), q_t(h), preferred_element_type=F32)
        if bias is not None:
            s = s + bias
        s_refs[h % 2][pl.ds(r0, tk), :] = s
        return jnp.maximum(m8, _col_reduce8(s, jnp.max))

    def value_pass(h, r0, m, l8, acc):
        p = jnp.exp2(s_refs[h % 2][pl.ds(r0, tk), :] - m)
        l8 = l8 + _col_reduce8(p, jnp.sum)
        acc = acc + jnp.dot(vt_tile(h, r0), p.astype(BF16), preferred_element_type=F32)
        return l8, acc

    def tile_step(h_score, h_value, m, r0, carry, bias):
        m8, l8, acc = carry
        if h_score is not None:
            m8 = score_pass(h_score, r0, m8, bias)
        if h_value is not None:
            l8, acc = value_pass(h_value, r0, m, l8, acc)
        return m8, l8, acc

    m = None
    for h in range(nheads + 1):
        h_score = h if h < nheads else None
        h_value = h - 1 if h > 0 else None
        carry = (jnp.full((8, tq), NEG_INF, F32), jnp.zeros((8, tq), F32), jnp.zeros((dv, tq), F32))
        carry = _for_tiles(
            nt, tk,
            lambda r0, c: tile_step(h_score, h_value, m, r0, c,
                                    None if (bias_tile is None or h_score is None) else bias_tile(r0)),
            carry)
        if diag_bias is not None:
            carry = tile_step(h_score, h_value, m, pl.multiple_of(nt * tk, tk), carry, diag_bias)
        m8, l8, acc = carry
        if h_value is not None:
            write_out(h_value, acc / jnp.sum(l8, axis=0, keepdims=True))
        m = jnp.max(m8, axis=0, keepdims=True)


def _transpose32(words):
    a = list(words)
    j, mask = 16, 0x0000FFFF
    while j:
        m = np.array(mask, np.uint32).view(np.int32)[()]
        for k in range(32):
            if not k & j:
                t = (lax.shift_right_logical(a[k], np.int32(j)) ^ a[k + j]) & m
                a[k + j] = a[k + j] ^ t
                a[k] = a[k] ^ lax.shift_left(t, np.int32(j))
        j >>= 1
        mask ^= (mask << j) & 0xFFFFFFFF
    return a


def _dsa_kernel(iqt_ref, iwt_ref, ik_ref, qt_ref, k_ref, vt_ref, ya_out,
                keys_ref, planes_ref, cand_ref, s0_ref, s1_ref, ot_ref, thr_ref, jcut_ref, trim_ref,
                *, tq, seq, ktop):
    tk = tq
    j = pl.program_id(1)
    nt = j + 1

    def put_keys(r0, diagonal):
        for sb in range(tk // IDX_ROWS):
            rs = pl.multiple_of(r0 + sb * IDX_ROWS, IDX_ROWS)
            ikt = ik_ref[0, pl.ds(rs, IDX_ROWS), :]
            acc = jnp.zeros((IDX_ROWS, tq), F32)
            for hd in range(IDX_HEADS):
                sh = jnp.dot(ikt, iqt_ref[0, hd * IDX_DIM:(hd + 1) * IDX_DIM, :],
                             preferred_element_type=F32)
                acc = acc + jnp.maximum(sh, 0.0) * iwt_ref[0, hd:hd + 1, :]
            bits = lax.bitcast_convert_type(acc, I32)
            key = bits ^ ((bits >> 31) & np.int32(0x7FFFFFFF))
            if diagonal:
                kc = (sb * IDX_ROWS + lax.broadcasted_iota(I32, (IDX_ROWS, tq), 0)) // CHUNK
                qc = lax.broadcasted_iota(I32, (IDX_ROWS, tq), 1) // CHUNK
                key = jnp.where(kc <= qc, key, INT_MIN)
            keys_ref[pl.ds(rs, IDX_ROWS), :] = key

    def idx_tile(kt, c):
        put_keys(pl.multiple_of(kt * tk, tk), False)
        return c

    lax.fori_loop(0, j, idx_tile, 0)
    put_keys(pl.multiple_of(j * tk, tk), True)

    def count(pred):
        def body(r0, c8):
            kk = keys_ref[pl.ds(r0, tk), :]
            row = r0 + lax.broadcasted_iota(I32, (tk, tq), 0)
            return c8 + _col_reduce8(pred(kk, row).astype(I32), jnp.sum)
        c8 = _for_tiles(nt, tk, body, jnp.zeros((8, tq), I32))
        return jnp.sum(c8, axis=0, keepdims=True)

    keep_all = INT_MIN + np.int32(1)

    @pl.when(nt * tk <= ktop)
    def _dense():
        thr_ref[...] = jnp.full((1, tq), keep_all, I32)
        trim_ref[0] = 0

    @pl.when(nt * tk > ktop)
    def _topk():
        def build_planes(g, c):
            rb = pl.multiple_of(g * PLANE_ROWS, PLANE_ROWS)
            wr = pl.multiple_of(g * 8, 8)
            for cs in range(tq // 128):
                lanes = slice(cs * 128, (cs + 1) * 128)
                words = [keys_ref[pl.ds(rb + 8 * i, 8), lanes] ^ INT_MIN for i in range(32)]
                for b, plane in enumerate(_transpose32(words)):
                    planes_ref[b, pl.ds(wr, 8), lanes] = plane
            return c

        lax.fori_loop(0, nt * (tk // PLANE_ROWS), build_planes, 0)

        nw = tk // 32

        def fill_cand(r0, c):
            cand_ref[pl.ds(r0, nw), :] = jnp.full((nw, tq), -1, I32)
            return c

        _for_tiles(nt, nw, fill_cand, 0)

        def ones_count(words_of):
            def body(r0, c8):
                pc = lax.population_count(words_of(r0))
                for q in range(nw // 8):
                    c8 = c8 + pc[q * 8:(q + 1) * 8]
                return c8
            c8 = _for_tiles(nt, nw, body, jnp.zeros((8, tq), I32))
            return jnp.sum(c8, axis=0, keepdims=True)

        def bit_step(i, carry):
            thr_u, left = carry
            b = 31 - i
            ones = ones_count(lambda r0: cand_ref[pl.ds(r0, nw), :] & planes_ref[b, pl.ds(r0, nw), :])
            take = ones >= left

            def narrow(r0, c):
                m = cand_ref[pl.ds(r0, nw), :]
                t = m & planes_ref[b, pl.ds(r0, nw), :]
                cand_ref[pl.ds(r0, nw), :] = jnp.where(take, t, m ^ t)
                return c

            _for_tiles(nt, nw, narrow, 0)
            return (jnp.where(take, thr_u | lax.shift_left(np.int32(1), b), thr_u),
                    jnp.where(take, left, left - ones))

        thr_u, need = lax.fori_loop(0, 32, bit_step,
                                    (jnp.zeros((1, tq), I32), jnp.full((1, tq), ktop, I32)))
        n_tied = ones_count(lambda r0: cand_ref[pl.ds(r0, nw), :])
        qpos = j * tq + lax.broadcasted_iota(I32, (1, tq), 1)
        all_kept = (qpos // CHUNK + 1) * CHUNK <= ktop
        thr = jnp.where(all_kept, keep_all, thr_u ^ INT_MIN)
        thr_ref[...] = thr
        trim_ref[0] = 0

        @pl.when(jnp.max(jnp.where(all_kept, 0, n_tied - need)) > 0)
        def _ties():
            nbits = int(seq - 1).bit_length()

            def row_step(i, jp):
                cand = jp | lax.shift_left(np.int32(1), nbits - 1 - i)
                c = count(lambda kk, row: (kk == thr) & (row < cand))
                return jnp.where(c < need, cand, jp)

            jcut_ref[...] = lax.fori_loop(0, nbits, row_step, jnp.zeros((1, tq), I32)) + 1
            trim_ref[0] = 1

    thr = thr_ref[...]

    def put_mask(r0, sel):
        keys_ref[pl.ds(r0, tk), :] = lax.bitcast_convert_type(
            jnp.where(sel, 0.0, NEG_INF).astype(F32), I32)

    @pl.when(trim_ref[0] == 0)
    def _mask_plain():
        def fill(r0, c):
            put_mask(r0, keys_ref[pl.ds(r0, tk), :] >= thr)
            return c
        _for_tiles(nt, tk, fill, 0)

    @pl.when(trim_ref[0] != 0)
    def _mask_trimmed():
        jcut = jcut_ref[...]

        def fill(r0, c):
            kk = keys_ref[pl.ds(r0, tk), :]
            row = r0 + lax.broadcasted_iota(I32, (tk, tq), 0)
            put_mask(r0, (kk > thr) | ((kk == thr) & (row < jcut)))
            return c
        _for_tiles(nt, tk, fill, 0)

    def write_out(hd, o):
        ot_ref[hd * A_HD:(hd + 1) * A_HD, :] = o

    _attention_heads(
        A_HEADS, nt, tk, tq, A_HD,
        lambda hd, r0: k_ref[0, pl.ds(r0, tk), :],
        lambda hd: qt_ref[0, hd * A_HD:(hd + 1) * A_HD, :],
        lambda hd, r0: vt_ref[0, :, pl.ds(r0, tk)],
        (s0_ref, s1_ref), write_out,
        bias_tile=lambda r0: lax.bitcast_convert_type(keys_ref[pl.ds(r0, tk), :], F32))
    ya_out[0] = ot_ref[...].T.astype(BF16)


def _dsa_call(iqt, iwt, ik, qt, k, vt, tq):
    b, _, s = qt.shape
    ktop = min(TOPK_MAX, s // 4)
    assert s % tq == 0 and tq % CHUNK == 0 and ktop % CHUNK == 0
    grid = (b, s // tq)
    qblk = lambda r: pl.BlockSpec((1, r, tq), lambda bi, j: (bi, 0, j))
    in_specs = [
        qblk(IDX_HEADS * IDX_DIM), qblk(IDX_HEADS),
        pl.BlockSpec((1, s, IDX_DIM), lambda bi, j: (bi, 0, 0)),
        qblk(A_WIDTH),
        pl.BlockSpec((1, s, A_HD), lambda bi, j: (bi, 0, 0)),
        pl.BlockSpec((1, A_HD, s), lambda bi, j: (bi, 0, 0)),
    ]
    return pl.pallas_call(
        functools.partial(_dsa_kernel, tq=tq, seq=s, ktop=ktop),
        grid=grid, in_specs=in_specs,
        out_specs=pl.BlockSpec((1, tq, A_WIDTH), lambda bi, j: (bi, j, 0)),
        out_shape=jax.ShapeDtypeStruct((b, s, A_WIDTH), BF16),
        scratch_shapes=[pltpu.VMEM((s, tq), I32), pltpu.VMEM((32, s // 32, tq), I32),
                        pltpu.VMEM((s // 32, tq), I32),
                        pltpu.VMEM((s, tq), F32), pltpu.VMEM((s, tq), F32),
                        pltpu.VMEM((A_WIDTH, tq), F32), pltpu.VMEM((1, tq), I32), pltpu.VMEM((1, tq), I32),
                        pltpu.SMEM((1,), I32)],
        compiler_params=_cparams(("parallel", "arbitrary")), name="dsa",
    )(iqt, iwt, ik, qt, k, vt)


def _mla_kernel(cqt_ref, ck_ref, cvt_ref, yc_out, s0_ref, s1_ref, ot_ref, *, tq):
    tk = tq
    j = pl.program_id(1)

    def write_out(hd, o):
        ot_ref[hd * C_VDIM:(hd + 1) * C_VDIM, :] = o

    _attention_heads(
        C_HEADS, j, tk, tq, C_VDIM,
        lambda hd, r0: ck_ref[0, hd, pl.ds(r0, tk), :],
        lambda hd: cqt_ref[0, hd * C_QK_PAD:(hd + 1) * C_QK_PAD, :],
        lambda hd, r0: cvt_ref[0, hd * C_VDIM:(hd + 1) * C_VDIM, pl.ds(r0, tk)],
        (s0_ref, s1_ref), write_out, diag_bias=_chunk_causal_bias(tk, tq))
    yc_out[0] = ot_ref[...].T.astype(BF16)


def _mla_call(cqt, ck, cvt, tq):
    b, _, s = cqt.shape
    assert s % tq == 0 and tq % CHUNK == 0
    grid = (b, s // tq)
    in_specs = [
        pl.BlockSpec((1, C_HEADS * C_QK_PAD, tq), lambda bi, j: (bi, 0, j)),
        pl.BlockSpec((1, C_HEADS, s, C_QK_PAD), lambda bi, j: (bi, 0, 0, 0)),
        pl.BlockSpec((1, C_WIDTH, s), lambda bi, j: (bi, 0, 0)),
    ]
    return pl.pallas_call(
        functools.partial(_mla_kernel, tq=tq),
        grid=grid, in_specs=in_specs,
        out_specs=pl.BlockSpec((1, tq, C_WIDTH), lambda bi, j: (bi, j, 0)),
        out_shape=jax.ShapeDtypeStruct((b, s, C_WIDTH), BF16),
        scratch_shapes=[pltpu.VMEM((s, tq), F32), pltpu.VMEM((s, tq), F32), pltpu.VMEM((C_WIDTH, tq), F32)],
        compiler_params=_cparams(("parallel", "arbitrary")), name="mla",
    )(cqt, ck, cvt)


def _s5_kernel(u_ref, bblk_ref, cre_ref, cim_ref, lam_ref, dskip_ref, wglu_ref, yb_out,
               state_ref, bb_ref, xs_ref, *, nb, tsteps):
    @pl.when(pl.program_id(0) == 0)
    def _init():
        state_ref[...] = jnp.zeros_like(state_ref)

    ar, ai, step = lam_ref[0:1, :], lam_ref[1:2, :], jnp.exp(lam_ref[2:3, :])
    mag = jnp.exp(ar * step)
    abar_re = mag * jnp.cos(ai * step)
    abar_im = mag * jnp.sin(ai * step)
    den = ar * ar + ai * ai
    nr = abar_re - 1.0
    f_re = (nr * ar + abar_im * ai) / den
    f_im = (abar_im * ar - nr * ai) / den

    u = u_ref[...]
    bu = jnp.dot(u.astype(BF16), bblk_ref[...], preferred_element_type=F32)
    bu_re, bu_im = bu[:, :S5_CH], bu[:, S5_CH:]
    bb_ref[:, :S5_CH] = f_re * bu_re - f_im * bu_im
    bb_ref[:, S5_CH:] = f_re * bu_im + f_im * bu_re

    a_re = jnp.broadcast_to(abar_re, (nb, S5_CH))
    a_im = jnp.broadcast_to(abar_im, (nb, S5_CH))

    def scan_step(t, carry):
        xr, xi = carry
        r0 = pl.multiple_of(t * nb, nb)
        nxr = a_re * xr - a_im * xi + bb_ref[pl.ds(r0, nb), :S5_CH]
        nxi = a_re * xi + a_im * xr + bb_ref[pl.ds(r0, nb), S5_CH:]
        xs_ref[pl.ds(r0, nb), :S5_CH] = nxr
        xs_ref[pl.ds(r0, nb), S5_CH:] = nxi
        return nxr, nxi

    xr, xi = lax.fori_loop(0, tsteps, scan_step, (state_ref[:, :S5_CH], state_ref[:, S5_CH:]))
    state_ref[:, :S5_CH] = xr
    state_ref[:, S5_CH:] = xi

    y = (jnp.dot(xs_ref[:, :S5_CH].astype(BF16), cre_ref[...], preferred_element_type=F32)
         - jnp.dot(xs_ref[:, S5_CH:].astype(BF16), cim_ref[...], preferred_element_type=F32)
         + dskip_ref[...] * u)
    g = 0.5 * y * (1.0 + jnp.tanh(np.float32(np.sqrt(2.0 / np.pi)) * (y + 0.044715 * (y * y * y))))
    gate = jax.nn.sigmoid(jnp.dot(g.astype(BF16), wglu_ref[...], preferred_element_type=F32))
    yb_out[...] = (g * gate).astype(BF16)


def _s5_call(u2d, bblk, cre, cim, lam, dskip, wglu, nb, tsteps):
    rows = u2d.shape[0]
    blk = nb * tsteps
    assert rows % blk == 0 and nb % 8 == 0
    full = lambda a: pl.BlockSpec(a.shape, lambda i: (0,) * a.ndim)
    return pl.pallas_call(
        functools.partial(_s5_kernel, nb=nb, tsteps=tsteps),
        grid=(rows // blk,),
        in_specs=[pl.BlockSpec((blk, S5_WIDTH), lambda i: (i, 0)),
                  full(bblk), full(cre), full(cim), full(lam), full(dskip), full(wglu)],
        out_specs=pl.BlockSpec((blk, S5_WIDTH), lambda i: (i, 0)),
        out_shape=jax.ShapeDtypeStruct((rows, S5_WIDTH), BF16),
        scratch_shapes=[pltpu.VMEM((nb, 2 * S5_CH), F32), pltpu.VMEM((blk, 2 * S5_CH), F32),
                        pltpu.VMEM((blk, 2 * S5_CH), F32)],
        compiler_params=_cparams(("arbitrary",)), name="s5",
    )(u2d, bblk, cre, cim, lam, dskip, wglu)


def _out_kernel(x_ref, g_ref, ya_ref, yb_ref, yc_ref, wo_ref, o_ref):
    g = g_ref[0].astype(F32)
    sg = g * jax.nn.sigmoid(g)
    ma = (ya_ref[0].astype(F32) * sg[:, :A_WIDTH]).astype(BF16)
    mb = (yb_ref[...].astype(F32) * sg[:, A_WIDTH:A_WIDTH + S5_WIDTH]).astype(BF16)
    mc = (yc_ref[0].astype(F32) * sg[:, A_WIDTH + S5_WIDTH:]).astype(BF16)
    o_ref[0] = (x_ref[0]
                + jnp.dot(ma, wo_ref[:A_WIDTH, :], preferred_element_type=F32)
                + jnp.dot(mb, wo_ref[A_WIDTH:A_WIDTH + S5_WIDTH, :], preferred_element_type=F32)
                + jnp.dot(mc, wo_ref[A_WIDTH + S5_WIDTH:, :], preferred_element_type=F32))


def _out_call(x, g, ya, yb, yc, wo, ts):
    b, s, d = x.shape
    tok = lambda w: pl.BlockSpec((1, ts, w), lambda bi, i: (bi, i, 0))
    return pl.pallas_call(
        _out_kernel, grid=(b, s // ts),
        in_specs=[tok(d), tok(1024), tok(A_WIDTH),
                  pl.BlockSpec((ts, S5_WIDTH), lambda bi, i: (i, bi)), tok(C_WIDTH),
                  pl.BlockSpec(wo.shape, lambda bi, i: (0, 0))],
        out_specs=tok(d), out_shape=jax.ShapeDtypeStruct((b, s, d), F32),
        compiler_params=_cparams(("parallel", "parallel")), name="outproj",
    )(x, g, ya, yb, yc, wo)


def _rope_table(s):
    pos = jnp.arange(s, dtype=jnp.int32).astype(F32)

    def cs(d):
        half = d // 2
        inv = ROPE_THETA ** (-jnp.arange(half, dtype=F32) * 2.0 / d)
        ang = pos[:, None] * inv[None, :]
        return jnp.cos(ang).T, jnp.sin(ang).T

    c64, s64 = cs(A_HD)
    c32, s32 = cs(IDX_DIM)
    return jnp.concatenate([c64, s64, c32, s32], axis=0)


def _block_diag(w):
    g, r, c = w.shape
    eye = jnp.eye(g, dtype=w.dtype)
    return (w[:, :, None, :] * eye[:, None, :, None]).reshape(g * r, g * c)


def _layer_operands(l, norm_g, w_in, attn_q_norm, attn_k_norm, mla_q_lora_norm, mla_kv_lora_norm,
                    mla_w_uq, mla_w_ukv, mla_q_norm, mla_k_norm, ssm_a_re, ssm_a_im, ssm_b_re,
                    ssm_b_im, ssm_c_re, ssm_c_im, ssm_d, ssm_log_step, ssm_w_glu, w_out):
    pts = [int(v) for v in np.cumsum(SPLIT_SIZES)[:-1]]
    (qa, ka, va, iq, ik, iw, ga, u, gb, cq, ckv, kpe, gc) = jnp.split(w_in[l], pts, axis=-1)
    wn = jnp.concatenate([ga, gb, gc, u], axis=1).astype(BF16)
    pad = jnp.zeros((w_in.shape[1], T_END - T_IW - IDX_HEADS), w_in.dtype)
    wt = jnp.concatenate([qa, ka, va, iq, ik, cq, ckv, kpe, iw, pad], axis=1).T.astype(BF16)
    gcol = jnp.concatenate([attn_q_norm[l], attn_k_norm[l], mla_q_lora_norm[l], mla_kv_lora_norm[l],
                            mla_q_norm[l], mla_k_norm[l]]).astype(F32)[:, None]
    wuq = mla_w_uq[l].T.reshape(C_HEADS, C_QK, Q_LORA)
    wuq = jnp.pad(wuq, ((0, 0), (0, C_QK_PAD - C_QK), (0, 0))).reshape(C_HEADS * C_QK_PAD, Q_LORA).astype(BF16)
    wukv = mla_w_ukv[l].T.astype(BF16)
    bblk = jnp.concatenate([_block_diag(jnp.swapaxes(ssm_b_re[l], 1, 2)),
                            _block_diag(jnp.swapaxes(ssm_b_im[l], 1, 2))], axis=1).astype(BF16)
    cre = _block_diag(jnp.swapaxes(ssm_c_re[l], 1, 2)).astype(BF16)
    cim = _block_diag(jnp.swapaxes(ssm_c_im[l], 1, 2)).astype(BF16)
    lam = jnp.stack([ssm_a_re[l].reshape(-1), ssm_a_im[l].reshape(-1),
                     jnp.repeat(ssm_log_step[l], S5_STATE)]).astype(F32)
    lam = jnp.concatenate([lam, jnp.zeros((5, S5_CH), F32)], axis=0)
    dskip = ssm_d[l].reshape(1, S5_WIDTH).astype(F32)
    return dict(ng=norm_g[l][None, :].astype(F32), wn=wn, wt=wt, gcol=gcol, wuq=wuq, wukv=wukv,
                bblk=bblk, cre=cre, cim=cim, lam=lam, dskip=dskip, wglu=ssm_w_glu[l].astype(BF16),
                wo=w_out[l].astype(BF16))


def kernel(x, norm_g, w_in, attn_q_norm, attn_k_norm, mla_q_lora_norm, mla_kv_lora_norm, mla_w_uq, mla_w_ukv, mla_q_norm, mla_k_norm, ssm_a_re, ssm_a_im, ssm_b_re, ssm_b_im, ssm_c_re, ssm_c_im, ssm_d, ssm_log_step, ssm_w_glu, w_out):
    b, s, _ = x.shape
    ts = min(512, s)
    tq = min(512, s)
    s5_steps = 32
    rope = _rope_table(s)
    params = (norm_g, w_in, attn_q_norm, attn_k_norm, mla_q_lora_norm, mla_kv_lora_norm, mla_w_uq,
              mla_w_ukv, mla_q_norm, mla_k_norm, ssm_a_re, ssm_a_im, ssm_b_re, ssm_b_im, ssm_c_re,
              ssm_c_im, ssm_d, ssm_log_step, ssm_w_glu, w_out)
    for l in range(w_in.shape[0]):
        p = _layer_operands(l, *params)
        (g, u, qt, k, vt, iqt, ik, iwt, cqt, ck, cvt) = _prep_call(
            x, p["ng"], p["wn"], p["wt"], p["gcol"], rope, p["wuq"], p["wukv"], ts)
        ya = _dsa_call(iqt, iwt, ik, qt, k, vt, tq)
        yc = _mla_call(cqt, ck, cvt, tq)
        yb = _s5_call(u.reshape(s * b, S5_WIDTH), p["bblk"], p["cre"], p["cim"], p["lam"],
                      p["dskip"], p["wglu"], b, s5_steps)
        x = _out_call(x, g, ya, yb.reshape(s, b * S5_WIDTH), yc, p["wo"], ts)
    return x
```

```python
import functools

import numpy as np
import jax
import jax.numpy as jnp
from jax import lax
from jax.experimental import pallas as pl
from jax.experimental.pallas import tpu as pltpu

F32 = jnp.float32
BF16 = jnp.bfloat16
I32 = jnp.int32

TILE_UNROLL = 2
IDX_ROWS = 128
PLANE_ROWS = 256
LOG2E = float(np.log2(np.e))

CHUNK = 64
ROPE_THETA = 10000.0
EPS = 1e-6

A_HEADS = 6
A_HD = 64
A_WIDTH = A_HEADS * A_HD
IDX_HEADS = 8
IDX_DIM = 32
TOPK_MAX = 256

S5_GROUP = 16
S5_WIDTH = 256
S5_GROUPS = S5_WIDTH // S5_GROUP
S5_STATE = 64
S5_CH = S5_GROUPS * S5_STATE

C_HEADS = 6
C_NOPE = 64
C_ROPE = 32
C_VDIM = 64
C_QK = C_NOPE + C_ROPE
C_QK_PAD = 128
C_WIDTH = C_HEADS * C_VDIM
Q_LORA = 256
KV_LORA = 128

SPLIT_SIZES = (A_WIDTH, A_HD, A_HD, IDX_HEADS * IDX_DIM, IDX_DIM, IDX_HEADS, A_WIDTH,
               S5_WIDTH, S5_WIDTH, Q_LORA, KV_LORA, C_ROPE, C_WIDTH)

T_QA, T_KA, T_VA, T_IQ, T_IK, T_CQ, T_CKV, T_KPE, T_IW, T_END = (
    0, 384, 448, 512, 768, 800, 1056, 1184, 1216, 1232)
G_AQ, G_AK, G_QL, G_KVL, G_MQ, G_MK, G_END = 0, 64, 128, 384, 512, 608, 704
R_C64, R_S64, R_C32, R_S32, R_END = 0, 32, 64, 80, 96

INT_MIN = np.int32(-2 ** 31)
NEG_INF = float("-inf")

VMEM_LIMIT = 56 * 1024 * 1024


def _cparams(sem):
    return pltpu.CompilerParams(dimension_semantics=sem, vmem_limit_bytes=VMEM_LIMIT)


def _rms_rows(x, gcol):
    r = lax.rsqrt(jnp.mean(x * x, axis=0, keepdims=True) + EPS)
    return x * r * gcol


def _rope_rows(x, cos, sin):
    half = x.shape[0] // 2
    x1, x2 = x[:half], x[half:]
    return jnp.concatenate([x1 * cos - x2 * sin, x2 * cos + x1 * sin], axis=0)


def _rows_to_token_major(xt, width):
    d, t = xt.shape
    if d < 128:
        xt = jnp.concatenate([xt, jnp.zeros((128 - d, t), F32)], axis=0)
    return xt.T[:, :width]


def _prep_kernel(x_ref, *refs):
    _prep_body(x_ref[0], *refs)


def _prep_body(x, ng_ref, wn_ref, wt_ref, gcol_ref, rope_ref, wuq_ref, wukv_ref,
               g_out, u_out, qt_out, k_out, vt_out, iqt_out, ik_out, iwt_out,
               cqt_out, ck_out, cvt_out):
    ms = jnp.mean(x * x, axis=-1, keepdims=True)
    h = (x * lax.rsqrt(ms + EPS) * ng_ref[...]).astype(BF16)

    pn = jnp.dot(h, wn_ref[...], preferred_element_type=F32)
    g_out[0] = pn[:, :1024].astype(BF16)
    u_out[...] = pn[:, 1024:1280]

    def tdot(lo, hi):
        return lax.dot_general(wt_ref[lo:hi, :], h, (((1,), (1,)), ((), ())),
                               preferred_element_type=F32)

    cos64 = rope_ref[R_C64:R_S64, :]
    sin64 = rope_ref[R_S64:R_C32, :]
    cos32 = rope_ref[R_C32:R_S32, :]
    sin32 = rope_ref[R_S32:R_END, :]

    pa = tdot(T_QA, T_IQ)
    g_aq = gcol_ref[G_AQ:G_AK, :]
    for hd in range(A_HEADS):
        qh = _rope_rows(_rms_rows(pa[hd * A_HD:(hd + 1) * A_HD], g_aq), cos64, sin64)
        qt_out[0, hd * A_HD:(hd + 1) * A_HD, :] = (qh * (LOG2E * A_HD ** -0.5)).astype(BF16)
    kt = _rope_rows(_rms_rows(pa[T_KA:T_VA], gcol_ref[G_AK:G_QL, :]), cos64, sin64)
    k_out[0] = _rows_to_token_major(kt, A_HD).astype(BF16)
    vt_out[0] = pa[T_VA:T_IQ].astype(BF16)

    pb = tdot(T_IQ, T_CQ)
    for hd in range(IDX_HEADS):
        iqh = _rope_rows(pb[hd * IDX_DIM:(hd + 1) * IDX_DIM], cos32, sin32)
        iqt_out[0, hd * IDX_DIM:(hd + 1) * IDX_DIM, :] = iqh.astype(BF16)
    ikt = _rope_rows(pb[T_IK - T_IQ:T_CQ - T_IQ], cos32, sin32)
    ik_out[0] = _rows_to_token_major(ikt, IDX_DIM).astype(BF16)

    pc = tdot(T_CQ, T_END)
    iwt_out[0] = pc[T_IW - T_CQ:T_IW - T_CQ + IDX_HEADS] * ((IDX_HEADS ** -0.5) * (IDX_DIM ** -0.5))

    cqn = _rms_rows(pc[:Q_LORA], gcol_ref[G_QL:G_KVL, :]).astype(BF16)
    qt = jnp.dot(wuq_ref[...], cqn, preferred_element_type=F32)
    g_mq = gcol_ref[G_MQ:G_MK, :]
    zpad = jnp.zeros((C_QK_PAD - C_QK, x.shape[0]), F32)
    for hd in range(C_HEADS):
        qh = _rms_rows(qt[hd * C_QK_PAD:hd * C_QK_PAD + C_QK], g_mq)
        qh = jnp.concatenate([qh[:C_NOPE], _rope_rows(qh[C_NOPE:], cos32, sin32), zpad], axis=0)
        cqt_out[0, hd * C_QK_PAD:(hd + 1) * C_QK_PAD, :] = (qh * (LOG2E * C_QK ** -0.5)).astype(BF16)

    ckvn = _rms_rows(pc[T_CKV - T_CQ:T_KPE - T_CQ], gcol_ref[G_KVL:G_MQ, :]).astype(BF16)
    kvt = jnp.dot(wukv_ref[...], ckvn, preferred_element_type=F32)
    kpe = pc[T_KPE - T_CQ:T_IW - T_CQ]
    kpe_ss = jnp.sum(kpe * kpe, axis=0, keepdims=True)
    g_mk = gcol_ref[G_MK:G_END, :]
    for hd in range(C_HEADS):
        kn = kvt[hd * 128:hd * 128 + C_NOPE]
        ss = jnp.sum(kn * kn, axis=0, keepdims=True) + kpe_ss
        r = lax.rsqrt(ss * (1.0 / C_QK) + EPS)
        kh = jnp.concatenate([kn * r * g_mk[:C_NOPE],
                              _rope_rows(kpe * r * g_mk[C_NOPE:], cos32, sin32), zpad], axis=0)
        ck_out[0, hd] = kh.T.astype(BF16)
        cvt_out[0, hd * C_VDIM:(hd + 1) * C_VDIM, :] = kvt[hd * 128 + C_NOPE:(hd + 1) * 128].astype(BF16)


def _prep_specs(b, s, ts, ng, wn, wt, gcol, wuq, wukv):
    tok = lambda w: pl.BlockSpec((1, ts, w), lambda bi, i: (bi, i, 0))
    feat = lambda r: pl.BlockSpec((1, r, ts), lambda bi, i: (bi, 0, i))
    full = lambda a: pl.BlockSpec(a.shape, lambda bi, i: (0,) * a.ndim)
    out_shape = (
        jax.ShapeDtypeStruct((b, s, 1024), BF16),
        jax.ShapeDtypeStruct((s, b * S5_WIDTH), F32),
        jax.ShapeDtypeStruct((b, A_WIDTH, s), BF16),
        jax.ShapeDtypeStruct((b, s, A_HD), BF16),
        jax.ShapeDtypeStruct((b, A_HD, s), BF16),
        jax.ShapeDtypeStruct((b, IDX_HEADS * IDX_DIM, s), BF16),
        jax.ShapeDtypeStruct((b, s, IDX_DIM), BF16),
        jax.ShapeDtypeStruct((b, IDX_HEADS, s), F32),
        jax.ShapeDtypeStruct((b, C_HEADS * C_QK_PAD, s), BF16),
        jax.ShapeDtypeStruct((b, C_HEADS, s, C_QK_PAD), BF16),
        jax.ShapeDtypeStruct((b, C_WIDTH, s), BF16),
    )
    out_specs = (
        tok(1024),
        pl.BlockSpec((ts, S5_WIDTH), lambda bi, i: (i, bi)),
        feat(A_WIDTH), tok(A_HD), feat(A_HD),
        feat(IDX_HEADS * IDX_DIM), tok(IDX_DIM), feat(IDX_HEADS),
        feat(C_HEADS * C_QK_PAD),
        pl.BlockSpec((1, C_HEADS, ts, C_QK_PAD), lambda bi, i: (bi, 0, i, 0)),
        feat(C_WIDTH),
    )
    in_specs = [full(ng), full(wn), full(wt), full(gcol),
                pl.BlockSpec((R_END, ts), lambda bi, i: (0, i)), full(wuq), full(wukv)]
    return in_specs, out_shape, out_specs


def _prep_call(x, ng, wn, wt, gcol, rope, wuq, wukv, ts):
    b, s, d = x.shape
    in_specs, out_shape, out_specs = _prep_specs(b, s, ts, ng, wn, wt, gcol, wuq, wukv)
    return pl.pallas_call(
        _prep_kernel, grid=(b, s // ts),
        in_specs=[pl.BlockSpec((1, ts, d), lambda bi, i: (bi, i, 0))] + in_specs,
        out_specs=out_specs, out_shape=out_shape,
        compiler_params=_cparams(("parallel", "parallel")), name="prep",
    )(x, ng, wn, wt, gcol, rope, wuq, wukv)


def _chunk_causal_bias(tk, tq):
    kc = lax.broadcasted_iota(I32, (tk, tq), 0) // CHUNK
    qc = lax.broadcasted_iota(I32, (tk, tq), 1) // CHUNK
    return jnp.where(kc <= qc, 0.0, NEG_INF).astype(F32)


def _col_reduce8(x, op):
    tk, tq = x.shape
    x = x.reshape(tk // 8, 8, tq)
    return op(x, axis=0)


def _for_tiles(nt, tk, body, carry):
    nfull = nt // TILE_UNROLL

    def group(i, c):
        for k in range(TILE_UNROLL):
            c = body(pl.multiple_of((i * TILE_UNROLL + k) * tk, tk), c)
        return c

    carry = lax.fori_loop(0, nfull, group, carry)
    return lax.fori_loop(nfull * TILE_UNROLL, nt, lambda kt, c: body(pl.multiple_of(kt * tk, tk), c), carry)


def _attention_heads(nheads, nt, tk, tq, dv, k_tile, q_t, vt_tile, s_refs, write_out,
                     bias_tile=None, diag_bias=None):
    def score_pass(h, r0, m8, bias):
        s = jnp.dot(k_tile(h, r0), q_t(h), preferred_element_type=F32)
        if bias is not None:
            s = s + bias
        s_refs[h % 2][pl.ds(r0, tk), :] = s
        return jnp.maximum(m8, _col_reduce8(s, jnp.max))

    def value_pass(h, r0, m, l8, acc):
        p = jnp.exp2(s_refs[h % 2][pl.ds(r0, tk), :] - m)
        l8 = l8 + _col_reduce8(p, jnp.sum)
        acc = acc + jnp.dot(vt_tile(h, r0), p.astype(BF16), preferred_element_type=F32)
        return l8, acc

    def tile_step(h_score, h_value, m, r0, carry, bias):
        m8, l8, acc = carry
        if h_score is not None:
            m8 = score_pass(h_score, r0, m8, bias)
        if h_value is not None:
            l8, acc = value_pass(h_value, r0, m, l8, acc)
        return m8, l8, acc

    m = None
    for h in range(nheads + 1):
        h_score = h if h < nheads else None
        h_value = h - 1 if h > 0 else None
        carry = (jnp.full((8, tq), NEG_INF, F32), jnp.zeros((8, tq), F32), jnp.zeros((dv, tq), F32))
        carry = _for_tiles(
            nt, tk,
            lambda r0, c: tile_step(h_score, h_value, m, r0, c,
                                    None if (bias_tile is None or h_score is None) else bias_tile(r0)),
            carry)
        if diag_bias is not None:
            carry = tile_step(h_score, h_value, m, pl.multiple_of(nt * tk, tk), carry, diag_bias)
        m8, l8, acc = carry
        if h_value is not None:
            write_out(h_value, acc / jnp.sum(l8, axis=0, keepdims=True))
        m = jnp.max(m8, axis=0, keepdims=True)


def _transpose32(words):
    a = list(words)
    j, mask = 16, 0x0000FFFF
    while j:
        m = np.array(mask, np.uint32).view(np.int32)[()]
        for k in range(32):
            if not k & j:
                t = (lax.shift_right_logical(a[k], np.int32(j)) ^ a[k + j]) & m
                a[k + j] = a[k + j] ^ t
                a[k] = a[k] ^ lax.shift_left(t, np.int32(j))
        j >>= 1
        mask ^= (mask << j) & 0xFFFFFFFF
    return a


def _dsa_kernel(iqt_ref, iwt_ref, ik_ref, qt_ref, k_ref, vt_ref, ya_out,
                keys_ref, planes_ref, cand_ref, s0_ref, s1_ref, ot_ref, thr_ref, jcut_ref, trim_ref,
                *, tq, seq, ktop):
    tk = tq
    j = pl.program_id(1)
    nt = j + 1

    def put_keys(r0, diagonal):
        for sb in range(tk // IDX_ROWS):
            rs = pl.multiple_of(r0 + sb * IDX_ROWS, IDX_ROWS)
            ikt = ik_ref[0, pl.ds(rs, IDX_ROWS), :]
            acc = jnp.zeros((IDX_ROWS, tq), F32)
            for hd in range(IDX_HEADS):
                sh = jnp.dot(ikt, iqt_ref[0, hd * IDX_DIM:(hd + 1) * IDX_DIM, :],
                             preferred_element_type=F32)
                acc = acc + jnp.maximum(sh, 0.0) * iwt_ref[0, hd:hd + 1, :]
            bits = lax.bitcast_convert_type(acc, I32)
            key = bits ^ ((bits >> 31) & np.int32(0x7FFFFFFF))
            if diagonal:
                kc = (sb * IDX_ROWS + lax.broadcasted_iota(I32, (IDX_ROWS, tq), 0)) // CHUNK
                qc = lax.broadcasted_iota(I32, (IDX_ROWS, tq), 1) // CHUNK
                key = jnp.where(kc <= qc, key, INT_MIN)
            keys_ref[pl.ds(rs, IDX_ROWS), :] = key

    def idx_tile(kt, c):
        put_keys(pl.multiple_of(kt * tk, tk), False)
        return c

    lax.fori_loop(0, j, idx_tile, 0)
    put_keys(pl.multiple_of(j * tk, tk), True)

    def count(pred):
        def body(r0, c8):
            kk = keys_ref[pl.ds(r0, tk), :]
            row = r0 + lax.broadcasted_iota(I32, (tk, tq), 0)
            return c8 + _col_reduce8(pred(kk, row).astype(I32), jnp.sum)
        c8 = _for_tiles(nt, tk, body, jnp.zeros((8, tq), I32))
        return jnp.sum(c8, axis=0, keepdims=True)

    keep_all = INT_MIN + np.int32(1)

    nw_all = seq // 32
    nw_tile = tk // 32
    if nw_all > nw_tile:
        @pl.when(j == 0)
        def _zero_tail():
            def zero_plane(b, c):
                planes_ref[b, nw_tile:, :] = jnp.zeros((nw_all - nw_tile, tq), I32)
                return c
            lax.fori_loop(0, 32, zero_plane, 0)

    @pl.when(nt * tk <= ktop)
    def _dense():
        thr_ref[...] = jnp.full((1, tq), keep_all, I32)
        trim_ref[0] = 0

    @pl.when(nt * tk > ktop)
    def _topk():
        def build_planes(g, c):
            rb = pl.multiple_of(g * PLANE_ROWS, PLANE_ROWS)
            wr = pl.multiple_of(g * 8, 8)
            for cs in range(tq // 128):
                lanes = slice(cs * 128, (cs + 1) * 128)
                words = [keys_ref[pl.ds(rb + 8 * i, 8), lanes] ^ INT_MIN for i in range(32)]
                for b, plane in enumerate(_transpose32(words)):
                    planes_ref[b, pl.ds(wr, 8), lanes] = plane
            return c

        lax.fori_loop(0, nt * (tk // PLANE_ROWS), build_planes, 0)

        word_row = lax.broadcasted_iota(I32, (nw_all, tq), 0)
        cand_ref[...] = jnp.where(word_row < nt * nw_tile, -1, 0).astype(I32)

        def ones_count(words):
            c8 = _col_reduce8(lax.population_count(words), jnp.sum)
            return jnp.sum(c8, axis=0, keepdims=True)

        def bit_step(i, carry):
            thr_u, left = carry
            b = 31 - i
            ones = ones_count(cand_ref[...] & planes_ref[b])
            take = ones >= left
            m = cand_ref[...]
            t = m & planes_ref[b]
            cand_ref[...] = jnp.where(take, t, m ^ t)
            return (jnp.where(take, thr_u | lax.shift_left(np.int32(1), b), thr_u),
                    jnp.where(take, left, left - ones))

        thr_u, need = lax.fori_loop(0, 32, bit_step,
                                    (jnp.zeros((1, tq), I32), jnp.full((1, tq), ktop, I32)))
        n_tied = ones_count(cand_ref[...])
        qpos = j * tq + lax.broadcasted_iota(I32, (1, tq), 1)
        all_kept = (qpos // CHUNK + 1) * CHUNK <= ktop
        thr = jnp.where(all_kept, keep_all, thr_u ^ INT_MIN)
        thr_ref[...] = thr
        trim_ref[0] = 0

        @pl.when(jnp.max(jnp.where(all_kept, 0, n_tied - need)) > 0)
        def _ties():
            nbits = int(seq - 1).bit_length()

            def row_step(i, jp):
                cand = jp | lax.shift_left(np.int32(1), nbits - 1 - i)
                c = count(lambda kk, row: (kk == thr) & (row < cand))
                return jnp.where(c < need, cand, jp)

            jcut_ref[...] = lax.fori_loop(0, nbits, row_step, jnp.zeros((1, tq), I32)) + 1
            trim_ref[0] = 1

    thr = thr_ref[...]

    def put_mask(r0, sel):
        keys_ref[pl.ds(r0, tk), :] = lax.bitcast_convert_type(
            jnp.where(sel, 0.0, NEG_INF).astype(F32), I32)

    @pl.when(trim_ref[0] == 0)
    def _mask_plain():
        def fill(r0, c):
            put_mask(r0, keys_ref[pl.ds(r0, tk), :] >= thr)
            return c
        _for_tiles(nt, tk, fill, 0)

    @pl.when(trim_ref[0] != 0)
    def _mask_trimmed():
        jcut = jcut_ref[...]

        def fill(r0, c):
            kk = keys_ref[pl.ds(r0, tk), :]
            row = r0 + lax.broadcasted_iota(I32, (tk, tq), 0)
            put_mask(r0, (kk > thr) | ((kk == thr) & (row < jcut)))
            return c
        _for_tiles(nt, tk, fill, 0)

    def write_out(hd, o):
        ot_ref[hd * A_HD:(hd + 1) * A_HD, :] = o

    _attention_heads(
        A_HEADS, nt, tk, tq, A_HD,
        lambda hd, r0: k_ref[0, pl.ds(r0, tk), :],
        lambda hd: qt_ref[0, hd * A_HD:(hd + 1) * A_HD, :],
        lambda hd, r0: vt_ref[0, :, pl.ds(r0, tk)],
        (s0_ref, s1_ref), write_out,
        bias_tile=lambda r0: lax.bitcast_convert_type(keys_ref[pl.ds(r0, tk), :], F32))
    ya_out[0] = ot_ref[...].T.astype(BF16)


def _dsa_call(iqt, iwt, ik, qt, k, vt, tq):
    b, _, s = qt.shape
    ktop = min(TOPK_MAX, s // 4)
    assert s % tq == 0 and tq % CHUNK == 0 and ktop % CHUNK == 0
    grid = (b, s // tq)
    qblk = lambda r: pl.BlockSpec((1, r, tq), lambda bi, j: (bi, 0, j))
    in_specs = [
        qblk(IDX_HEADS * IDX_DIM), qblk(IDX_HEADS),
        pl.BlockSpec((1, s, IDX_DIM), lambda bi, j: (bi, 0, 0)),
        qblk(A_WIDTH),
        pl.BlockSpec((1, s, A_HD), lambda bi, j: (bi, 0, 0)),
        pl.BlockSpec((1, A_HD, s), lambda bi, j: (bi, 0, 0)),
    ]
    return pl.pallas_call(
        functools.partial(_dsa_kernel, tq=tq, seq=s, ktop=ktop),
        grid=grid, in_specs=in_specs,
        out_specs=pl.BlockSpec((1, tq, A_WIDTH), lambda bi, j: (bi, j, 0)),
        out_shape=jax.ShapeDtypeStruct((b, s, A_WIDTH), BF16),
        scratch_shapes=[pltpu.VMEM((s, tq), I32), pltpu.VMEM((32, s // 32, tq), I32),
                        pltpu.VMEM((s // 32, tq), I32),
                        pltpu.VMEM((s, tq), F32), pltpu.VMEM((s, tq), F32),
                        pltpu.VMEM((A_WIDTH, tq), F32), pltpu.VMEM((1, tq), I32), pltpu.VMEM((1, tq), I32),
                        pltpu.SMEM((1,), I32)],
        compiler_params=_cparams(("parallel", "arbitrary")), name="dsa",
    )(iqt, iwt, ik, qt, k, vt)


def _mla_kernel(cqt_ref, ck_ref, cvt_ref, yc_out, s0_ref, s1_ref, ot_ref, *, tq):
    tk = tq
    j = pl.program_id(1)

    def write_out(hd, o):
        ot_ref[hd * C_VDIM:(hd + 1) * C_VDIM, :] = o

    _attention_heads(
        C_HEADS, j, tk, tq, C_VDIM,
        lambda hd, r0: ck_ref[0, hd, pl.ds(r0, tk), :],
        lambda hd: cqt_ref[0, hd * C_QK_PAD:(hd + 1) * C_QK_PAD, :],
        lambda hd, r0: cvt_ref[0, hd * C_VDIM:(hd + 1) * C_VDIM, pl.ds(r0, tk)],
        (s0_ref, s1_ref), write_out, diag_bias=_chunk_causal_bias(tk, tq))
    yc_out[0] = ot_ref[...].T.astype(BF16)


def _mla_call(cqt, ck, cvt, tq):
    b, _, s = cqt.shape
    assert s % tq == 0 and tq % CHUNK == 0
    grid = (b, s // tq)
    in_specs = [
        pl.BlockSpec((1, C_HEADS * C_QK_PAD, tq), lambda bi, j: (bi, 0, j)),
        pl.BlockSpec((1, C_HEADS, s, C_QK_PAD), lambda bi, j: (bi, 0, 0, 0)),
        pl.BlockSpec((1, C_WIDTH, s), lambda bi, j: (bi, 0, 0)),
    ]
    return pl.pallas_call(
        functools.partial(_mla_kernel, tq=tq),
        grid=grid, in_specs=in_specs,
        out_specs=pl.BlockSpec((1, tq, C_WIDTH), lambda bi, j: (bi, j, 0)),
        out_shape=jax.ShapeDtypeStruct((b, s, C_WIDTH), BF16),
        scratch_shapes=[pltpu.VMEM((s, tq), F32), pltpu.VMEM((s, tq), F32), pltpu.VMEM((C_WIDTH, tq), F32)],
        compiler_params=_cparams(("parallel", "arbitrary")), name="mla",
    )(cqt, ck, cvt)


def _s5_kernel(u_ref, bblk_ref, cre_ref, cim_ref, lam_ref, dskip_ref, wglu_ref, yb_out,
               state_ref, bb_ref, xs_ref, *, nb, tsteps):
    @pl.when(pl.program_id(0) == 0)
    def _init():
        state_ref[...] = jnp.zeros_like(state_ref)

    ar, ai, step = lam_ref[0:1, :], lam_ref[1:2, :], jnp.exp(lam_ref[2:3, :])
    mag = jnp.exp(ar * step)
    abar_re = mag * jnp.cos(ai * step)
    abar_im = mag * jnp.sin(ai * step)
    den = ar * ar + ai * ai
    nr = abar_re - 1.0
    f_re = (nr * ar + abar_im * ai) / den
    f_im = (abar_im * ar - nr * ai) / den

    u = u_ref[...]
    bu = jnp.dot(u.astype(BF16), bblk_ref[...], preferred_element_type=F32)
    bu_re, bu_im = bu[:, :S5_CH], bu[:, S5_CH:]
    bb_ref[:, :S5_CH] = f_re * bu_re - f_im * bu_im
    bb_ref[:, S5_CH:] = f_re * bu_im + f_im * bu_re

    a_re = jnp.broadcast_to(abar_re, (nb, S5_CH))
    a_im = jnp.broadcast_to(abar_im, (nb, S5_CH))

    def scan_step(t, carry):
        xr, xi = carry
        r0 = pl.multiple_of(t * nb, nb)
        nxr = a_re * xr - a_im * xi + bb_ref[pl.ds(r0, nb), :S5_CH]
        nxi = a_re * xi + a_im * xr + bb_ref[pl.ds(r0, nb), S5_CH:]
        xs_ref[pl.ds(r0, nb), :S5_CH] = nxr
        xs_ref[pl.ds(r0, nb), S5_CH:] = nxi
        return nxr, nxi

    xr, xi = lax.fori_loop(0, tsteps, scan_step, (state_ref[:, :S5_CH], state_ref[:, S5_CH:]))
    state_ref[:, :S5_CH] = xr
    state_ref[:, S5_CH:] = xi

    y = (jnp.dot(xs_ref[:, :S5_CH].astype(BF16), cre_ref[...], preferred_element_type=F32)
         - jnp.dot(xs_ref[:, S5_CH:].astype(BF16), cim_ref[...], preferred_element_type=F32)
         + dskip_ref[...] * u)
    g = 0.5 * y * (1.0 + jnp.tanh(np.float32(np.sqrt(2.0 / np.pi)) * (y + 0.044715 * (y * y * y))))
    gate = jax.nn.sigmoid(jnp.dot(g.astype(BF16), wglu_ref[...], preferred_element_type=F32))
    yb_out[...] = (g * gate).astype(BF16)


def _s5_call(u2d, bblk, cre, cim, lam, dskip, wglu, nb, tsteps):
    rows = u2d.shape[0]
    blk = nb * tsteps
    assert rows % blk == 0 and nb % 8 == 0
    full = lambda a: pl.BlockSpec(a.shape, lambda i: (0,) * a.ndim)
    return pl.pallas_call(
        functools.partial(_s5_kernel, nb=nb, tsteps=tsteps),
        grid=(rows // blk,),
        in_specs=[pl.BlockSpec((blk, S5_WIDTH), lambda i: (i, 0)),
                  full(bblk), full(cre), full(cim), full(lam), full(dskip), full(wglu)],
        out_specs=pl.BlockSpec((blk, S5_WIDTH), lambda i: (i, 0)),
        out_shape=jax.ShapeDtypeStruct((rows, S5_WIDTH), BF16),
        scratch_shapes=[pltpu.VMEM((nb, 2 * S5_CH), F32), pltpu.VMEM((blk, 2 * S5_CH), F32),
                        pltpu.VMEM((blk, 2 * S5_CH), F32)],
        compiler_params=_cparams(("arbitrary",)), name="s5",
    )(u2d, bblk, cre, cim, lam, dskip, wglu)


def _out_body(x_ref, g_ref, ya_ref, yb_ref, yc_ref, wo_ref):
    g = g_ref[0].astype(F32)
    sg = g * jax.nn.sigmoid(g)
    ma = (ya_ref[0].astype(F32) * sg[:, :A_WIDTH]).astype(BF16)
    mb = (yb_ref[...].astype(F32) * sg[:, A_WIDTH:A_WIDTH + S5_WIDTH]).astype(BF16)
    mc = (yc_ref[0].astype(F32) * sg[:, A_WIDTH + S5_WIDTH:]).astype(BF16)
    return (x_ref[0]
            + jnp.dot(ma, wo_ref[:A_WIDTH, :], preferred_element_type=F32)
            + jnp.dot(mb, wo_ref[A_WIDTH:A_WIDTH + S5_WIDTH, :], preferred_element_type=F32)
            + jnp.dot(mc, wo_ref[A_WIDTH + S5_WIDTH:, :], preferred_element_type=F32))


def _out_kernel(x_ref, g_ref, ya_ref, yb_ref, yc_ref, wo_ref, o_ref):
    o_ref[0] = _out_body(x_ref, g_ref, ya_ref, yb_ref, yc_ref, wo_ref)


def _out_prep_kernel(x_ref, g_ref, ya_ref, yb_ref, yc_ref, wo_ref, *refs):
    prep_in, o_ref, prep_out = refs[:7], refs[7], refs[8:]
    x = _out_body(x_ref, g_ref, ya_ref, yb_ref, yc_ref, wo_ref)
    o_ref[0] = x
    _prep_body(x, *prep_in, *prep_out)


def _out_specs(d, ts, wo):
    tok = lambda w: pl.BlockSpec((1, ts, w), lambda bi, i: (bi, i, 0))
    return [tok(d), tok(1024), tok(A_WIDTH),
            pl.BlockSpec((ts, S5_WIDTH), lambda bi, i: (i, bi)), tok(C_WIDTH),
            pl.BlockSpec(wo.shape, lambda bi, i: (0, 0))]


def _out_call(x, g, ya, yb, yc, wo, ts):
    b, s, d = x.shape
    return pl.pallas_call(
        _out_kernel, grid=(b, s // ts), in_specs=_out_specs(d, ts, wo),
        out_specs=pl.BlockSpec((1, ts, d), lambda bi, i: (bi, i, 0)),
        out_shape=jax.ShapeDtypeStruct((b, s, d), F32),
        compiler_params=_cparams(("parallel", "parallel")), name="outproj",
    )(x, g, ya, yb, yc, wo)


def _out_prep_call(x, g, ya, yb, yc, wo, ng, wn, wt, gcol, rope, wuq, wukv, ts):
    b, s, d = x.shape
    in_specs, out_shape, out_specs = _prep_specs(b, s, ts, ng, wn, wt, gcol, wuq, wukv)
    xspec = pl.BlockSpec((1, ts, d), lambda bi, i: (bi, i, 0))
    return pl.pallas_call(
        _out_prep_kernel, grid=(b, s // ts),
        in_specs=_out_specs(d, ts, wo) + in_specs,
        out_specs=(xspec,) + tuple(out_specs),
        out_shape=(jax.ShapeDtypeStruct((b, s, d), F32),) + tuple(out_shape),
        compiler_params=_cparams(("parallel", "parallel")), name="outprep",
    )(x, g, ya, yb, yc, wo, ng, wn, wt, gcol, rope, wuq, wukv)


def _rope_table(s):
    pos = jnp.arange(s, dtype=jnp.int32).astype(F32)

    def cs(d):
        half = d // 2
        inv = ROPE_THETA ** (-jnp.arange(half, dtype=F32) * 2.0 / d)
        ang = pos[:, None] * inv[None, :]
        return jnp.cos(ang).T, jnp.sin(ang).T

    c64, s64 = cs(A_HD)
    c32, s32 = cs(IDX_DIM)
    return jnp.concatenate([c64, s64, c32, s32], axis=0)


def _block_diag(w):
    g, r, c = w.shape
    eye = jnp.eye(g, dtype=w.dtype)
    return (w[:, :, None, :] * eye[:, None, :, None]).reshape(g * r, g * c)


def _layer_operands(l, norm_g, w_in, attn_q_norm, attn_k_norm, mla_q_lora_norm, mla_kv_lora_norm,
                    mla_w_uq, mla_w_ukv, mla_q_norm, mla_k_norm, ssm_a_re, ssm_a_im, ssm_b_re,
                    ssm_b_im, ssm_c_re, ssm_c_im, ssm_d, ssm_log_step, ssm_w_glu, w_out):
    pts = [int(v) for v in np.cumsum(SPLIT_SIZES)[:-1]]
    (qa, ka, va, iq, ik, iw, ga, u, gb, cq, ckv, kpe, gc) = jnp.split(w_in[l], pts, axis=-1)
    wn = jnp.concatenate([ga, gb, gc, u], axis=1).astype(BF16)
    pad = jnp.zeros((w_in.shape[1], T_END - T_IW - IDX_HEADS), w_in.dtype)
    wt = jnp.concatenate([qa, ka, va, iq, ik, cq, ckv, kpe, iw, pad], axis=1).T.astype(BF16)
    gcol = jnp.concatenate([attn_q_norm[l], attn_k_norm[l], mla_q_lora_norm[l], mla_kv_lora_norm[l],
                            mla_q_norm[l], mla_k_norm[l]]).astype(F32)[:, None]
    wuq = mla_w_uq[l].T.reshape(C_HEADS, C_QK, Q_LORA)
    wuq = jnp.pad(wuq, ((0, 0), (0, C_QK_PAD - C_QK), (0, 0))).reshape(C_HEADS * C_QK_PAD, Q_LORA).astype(BF16)
    wukv = mla_w_ukv[l].T.astype(BF16)
    bblk = jnp.concatenate([_block_diag(jnp.swapaxes(ssm_b_re[l], 1, 2)),
                            _block_diag(jnp.swapaxes(ssm_b_im[l], 1, 2))], axis=1).astype(BF16)
    cre = _block_diag(jnp.swapaxes(ssm_c_re[l], 1, 2)).astype(BF16)
    cim = _block_diag(jnp.swapaxes(ssm_c_im[l], 1, 2)).astype(BF16)
    lam = jnp.stack([ssm_a_re[l].reshape(-1), ssm_a_im[l].reshape(-1),
                     jnp.repeat(ssm_log_step[l], S5_STATE)]).astype(F32)
    lam = jnp.concatenate([lam, jnp.zeros((5, S5_CH), F32)], axis=0)
    dskip = ssm_d[l].reshape(1, S5_WIDTH).astype(F32)
    return dict(ng=norm_g[l][None, :].astype(F32), wn=wn, wt=wt, gcol=gcol, wuq=wuq, wukv=wukv,
                bblk=bblk, cre=cre, cim=cim, lam=lam, dskip=dskip, wglu=ssm_w_glu[l].astype(BF16),
                wo=w_out[l].astype(BF16))


def kernel(x, norm_g, w_in, attn_q_norm, attn_k_norm, mla_q_lora_norm, mla_kv_lora_norm, mla_w_uq, mla_w_ukv, mla_q_norm, mla_k_norm, ssm_a_re, ssm_a_im, ssm_b_re, ssm_b_im, ssm_c_re, ssm_c_im, ssm_d, ssm_log_step, ssm_w_glu, w_out):
    b, s, _ = x.shape
    ts = min(512, s)
    tq = min(512, s)
    s5_steps = 32
    rope = _rope_table(s)
    params = (norm_g, w_in, attn_q_norm, attn_k_norm, mla_q_lora_norm, mla_kv_lora_norm, mla_w_uq,
              mla_w_ukv, mla_q_norm, mla_k_norm, ssm_a_re, ssm_a_im, ssm_b_re, ssm_b_im, ssm_c_re,
              ssm_c_im, ssm_d, ssm_log_step, ssm_w_glu, w_out)
    depth = w_in.shape[0]
    layers = [_layer_operands(l, *params) for l in range(depth)]
    prep_w = lambda p: (p["ng"], p["wn"], p["wt"], p["gcol"], rope, p["wuq"], p["wukv"])
    prepped = _prep_call(x, *prep_w(layers[0]), ts)
    for l, p in enumerate(layers):
        (g, u, qt, k, vt, iqt, ik, iwt, cqt, ck, cvt) = prepped
        ya = _dsa_call(iqt, iwt, ik, qt, k, vt, tq)
        yc = _mla_call(cqt, ck, cvt, tq)
        yb = _s5_call(u.reshape(s * b, S5_WIDTH), p["bblk"], p["cre"], p["cim"], p["lam"],
                      p["dskip"], p["wglu"], b, s5_steps).reshape(s, b * S5_WIDTH)
        if l + 1 < depth:
            x, *prepped = _out_prep_call(x, g, ya, yb, yc, p["wo"], *prep_w(layers[l + 1]), ts)
        else:
            x = _out_call(x, g, ya, yb, yc, p["wo"], ts)
    return x
```

```python
import functools

import numpy as np
import jax
import jax.numpy as jnp
from jax import lax
from jax.experimental import pallas as pl
from jax.experimental.pallas import tpu as pltpu

F32 = jnp.float32
BF16 = jnp.bfloat16
I32 = jnp.int32

TILE_UNROLL = 2
IDX_ROWS = 128
PLANE_ROWS = 256
LOG2E = float(np.log2(np.e))

CHUNK = 64
ROPE_THETA = 10000.0
EPS = 1e-6

A_HEADS = 6
A_HD = 64
A_WIDTH = A_HEADS * A_HD
IDX_HEADS = 8
IDX_DIM = 32
TOPK_MAX = 256

S5_GROUP = 16
S5_WIDTH = 256
S5_GROUPS = S5_WIDTH // S5_GROUP
S5_STATE = 64
S5_CH = S5_GROUPS * S5_STATE

C_HEADS = 6
C_NOPE = 64
C_ROPE = 32
C_VDIM = 64
C_QK = C_NOPE + C_ROPE
C_QK_PAD = 128
C_WIDTH = C_HEADS * C_VDIM
Q_LORA = 256
KV_LORA = 128

SPLIT_SIZES = (A_WIDTH, A_HD, A_HD, IDX_HEADS * IDX_DIM, IDX_DIM, IDX_HEADS, A_WIDTH,
               S5_WIDTH, S5_WIDTH, Q_LORA, KV_LORA, C_ROPE, C_WIDTH)

T_QA, T_KA, T_VA, T_IQ, T_IK, T_CQ, T_CKV, T_KPE, T_IW, T_END = (
    0, 384, 448, 512, 768, 800, 1056, 1184, 1216, 1232)
G_AQ, G_AK, G_QL, G_KVL, G_MQ, G_MK, G_END = 0, 64, 128, 384, 512, 608, 704
R_C64, R_S64, R_C32, R_S32, R_END = 0, 32, 64, 80, 96

INT_MIN = np.int32(-2 ** 31)
NEG_INF = float("-inf")

VMEM_LIMIT = 56 * 1024 * 1024


def _cparams(sem):
    return pltpu.CompilerParams(dimension_semantics=sem, vmem_limit_bytes=VMEM_LIMIT)


def _rms_rows(x, gcol):
    r = lax.rsqrt(jnp.mean(x * x, axis=0, keepdims=True) + EPS)
    return x * r * gcol


def _rope_rows(x, cos, sin):
    half = x.shape[0] // 2
    x1, x2 = x[:half], x[half:]
    return jnp.concatenate([x1 * cos - x2 * sin, x2 * cos + x1 * sin], axis=0)


def _rows_to_token_major(xt, width):
    d, t = xt.shape
    if d < 128:
        xt = jnp.concatenate([xt, jnp.zeros((128 - d, t), F32)], axis=0)
    return xt.T[:, :width]


def _prep_kernel(x_ref, *refs):
    _prep_body(x_ref[0], *refs)


def _prep_body(x, ng_ref, wn_ref, wt_ref, gcol_ref, rope_ref, wuq_ref, wukv_ref,
               g_out, u_out, qt_out, k_out, vt_out, iqt_out, ik_out, iwt_out,
               cqt_out, ck_out, cvt_out):
    ms = jnp.mean(x * x, axis=-1, keepdims=True)
    h = (x * lax.rsqrt(ms + EPS) * ng_ref[...]).astype(BF16)

    pn = jnp.dot(h, wn_ref[...], preferred_element_type=F32)
    g_out[0] = pn[:, :1024].astype(BF16)
    u_out[...] = pn[:, 1024:1280]

    def tdot(lo, hi):
        return lax.dot_general(wt_ref[lo:hi, :], h, (((1,), (1,)), ((), ())),
                               preferred_element_type=F32)

    cos64 = rope_ref[R_C64:R_S64, :]
    sin64 = rope_ref[R_S64:R_C32, :]
    cos32 = rope_ref[R_C32:R_S32, :]
    sin32 = rope_ref[R_S32:R_END, :]

    pa = tdot(T_QA, T_IQ)
    g_aq = gcol_ref[G_AQ:G_AK, :]
    for hd in range(A_HEADS):
        qh = _rope_rows(_rms_rows(pa[hd * A_HD:(hd + 1) * A_HD], g_aq), cos64, sin64)
        qt_out[0, hd * A_HD:(hd + 1) * A_HD, :] = (qh * (LOG2E * A_HD ** -0.5)).astype(BF16)
    kt = _rope_rows(_rms_rows(pa[T_KA:T_VA], gcol_ref[G_AK:G_QL, :]), cos64, sin64)
    k_out[0] = _rows_to_token_major(kt, A_HD).astype(BF16)
    vt_out[0] = pa[T_VA:T_IQ].astype(BF16)

    pb = tdot(T_IQ, T_CQ)
    for hd in range(IDX_HEADS):
        iqh = _rope_rows(pb[hd * IDX_DIM:(hd + 1) * IDX_DIM], cos32, sin32)
        iqt_out[0, hd * IDX_DIM:(hd + 1) * IDX_DIM, :] = iqh.astype(BF16)
    ikt = _rope_rows(pb[T_IK - T_IQ:T_CQ - T_IQ], cos32, sin32)
    ik_out[0] = _rows_to_token_major(ikt, IDX_DIM).astype(BF16)

    pc = tdot(T_CQ, T_END)
    iwt_out[0] = pc[T_IW - T_CQ:T_IW - T_CQ + IDX_HEADS] * ((IDX_HEADS ** -0.5) * (IDX_DIM ** -0.5))

    cqn = _rms_rows(pc[:Q_LORA], gcol_ref[G_QL:G_KVL, :]).astype(BF16)
    qt = jnp.dot(wuq_ref[...], cqn, preferred_element_type=F32)
    g_mq = gcol_ref[G_MQ:G_MK, :]
    zpad = jnp.zeros((C_QK_PAD - C_QK, x.shape[0]), F32)
    for hd in range(C_HEADS):
        qh = _rms_rows(qt[hd * C_QK_PAD:hd * C_QK_PAD + C_QK], g_mq)
        qh = jnp.concatenate([qh[:C_NOPE], _rope_rows(qh[C_NOPE:], cos32, sin32), zpad], axis=0)
        cqt_out[0, hd * C_QK_PAD:(hd + 1) * C_QK_PAD, :] = (qh * (LOG2E * C_QK ** -0.5)).astype(BF16)

    ckvn = _rms_rows(pc[T_CKV - T_CQ:T_KPE - T_CQ], gcol_ref[G_KVL:G_MQ, :]).astype(BF16)
    kvt = jnp.dot(wukv_ref[...], ckvn, preferred_element_type=F32)
    kpe = pc[T_KPE - T_CQ:T_IW - T_CQ]
    kpe_ss = jnp.sum(kpe * kpe, axis=0, keepdims=True)
    g_mk = gcol_ref[G_MK:G_END, :]
    for hd in range(C_HEADS):
        kn = kvt[hd * 128:hd * 128 + C_NOPE]
        ss = jnp.sum(kn * kn, axis=0, keepdims=True) + kpe_ss
        r = lax.rsqrt(ss * (1.0 / C_QK) + EPS)
        kh = jnp.concatenate([kn * r * g_mk[:C_NOPE],
                              _rope_rows(kpe * r * g_mk[C_NOPE:], cos32, sin32), zpad], axis=0)
        ck_out[0, hd] = kh.T.astype(BF16)
        cvt_out[0, hd * C_VDIM:(hd + 1) * C_VDIM, :] = kvt[hd * 128 + C_NOPE:(hd + 1) * 128].astype(BF16)


def _prep_specs(b, s, ts, ng, wn, wt, gcol, wuq, wukv):
    tok = lambda w: pl.BlockSpec((1, ts, w), lambda bi, i: (bi, i, 0))
    feat = lambda r: pl.BlockSpec((1, r, ts), lambda bi, i: (bi, 0, i))
    full = lambda a: pl.BlockSpec(a.shape, lambda bi, i: (0,) * a.ndim)
    out_shape = (
        jax.ShapeDtypeStruct((b, s, 1024), BF16),
        jax.ShapeDtypeStruct((s, b * S5_WIDTH), F32),
        jax.ShapeDtypeStruct((b, A_WIDTH, s), BF16),
        jax.ShapeDtypeStruct((b, s, A_HD), BF16),
        jax.ShapeDtypeStruct((b, A_HD, s), BF16),
        jax.ShapeDtypeStruct((b, IDX_HEADS * IDX_DIM, s), BF16),
        jax.ShapeDtypeStruct((b, s, IDX_DIM), BF16),
        jax.ShapeDtypeStruct((b, IDX_HEADS, s), F32),
        jax.ShapeDtypeStruct((b, C_HEADS * C_QK_PAD, s), BF16),
        jax.ShapeDtypeStruct((b, C_HEADS, s, C_QK_PAD), BF16),
        jax.ShapeDtypeStruct((b, C_WIDTH, s), BF16),
    )
    out_specs = (
        tok(1024),
        pl.BlockSpec((ts, S5_WIDTH), lambda bi, i: (i, bi)),
        feat(A_WIDTH), tok(A_HD), feat(A_HD),
        feat(IDX_HEADS * IDX_DIM), tok(IDX_DIM), feat(IDX_HEADS),
        feat(C_HEADS * C_QK_PAD),
        pl.BlockSpec((1, C_HEADS, ts, C_QK_PAD), lambda bi, i: (bi, 0, i, 0)),
        feat(C_WIDTH),
    )
    in_specs = [full(ng), full(wn), full(wt), full(gcol),
                pl.BlockSpec((R_END, ts), lambda bi, i: (0, i)), full(wuq), full(wukv)]
    return in_specs, out_shape, out_specs


def _prep_call(x, ng, wn, wt, gcol, rope, wuq, wukv, ts):
    b, s, d = x.shape
    in_specs, out_shape, out_specs = _prep_specs(b, s, ts, ng, wn, wt, gcol, wuq, wukv)
    return pl.pallas_call(
        _prep_kernel, grid=(b, s // ts),
        in_specs=[pl.BlockSpec((1, ts, d), lambda bi, i: (bi, i, 0))] + in_specs,
        out_specs=out_specs, out_shape=out_shape,
        compiler_params=_cparams(("parallel", "parallel")), name="prep",
    )(x, ng, wn, wt, gcol, rope, wuq, wukv)


def _chunk_causal_bias(tk, tq):
    kc = lax.broadcasted_iota(I32, (tk, tq), 0) // CHUNK
    qc = lax.broadcasted_iota(I32, (tk, tq), 1) // CHUNK
    return jnp.where(kc <= qc, 0.0, NEG_INF).astype(F32)


def _col_reduce8(x, op):
    tk, tq = x.shape
    x = x.reshape(tk // 8, 8, tq)
    return op(x, axis=0)


def _for_tiles(nt, tk, body, carry):
    nfull = nt // TILE_UNROLL

    def group(i, c):
        for k in range(TILE_UNROLL):
            c = body(pl.multiple_of((i * TILE_UNROLL + k) * tk, tk), c)
        return c

    carry = lax.fori_loop(0, nfull, group, carry)
    return lax.fori_loop(nfull * TILE_UNROLL, nt, lambda kt, c: body(pl.multiple_of(kt * tk, tk), c), carry)


def _attention_heads(nheads, nt, tk, tq, dv, k_tile, q_t, vt_tile, s_refs, write_out,
                     bias_tile=None, diag_bias=None):
    def score_pass(h, r0, m8, bias):
        s = jnp.dot(k_tile(h, r0), q_t(h), preferred_element_type=F32)
        if bias is not None:
            s = s + bias
        s_refs[h % 2][pl.ds(r0, tk), :] = s
        return jnp.maximum(m8, _col_reduce8(s, jnp.max))

    def value_pass(h, r0, m, l8, acc):
        p = jnp.exp2(s_refs[h % 2][pl.ds(r0, tk), :] - m)
        l8 = l8 + _col_reduce8(p, jnp.sum)
        acc = acc + jnp.dot(vt_tile(h, r0), p.astype(BF16), preferred_element_type=F32)
        return l8, acc

    def tile_step(h_score, h_value, m, r0, carry, bias):
        m8, l8, acc = carry
        if h_score is not None:
            m8 = score_pass(h_score, r0, m8, bias)
        if h_value is not None:
            l8, acc = value_pass(h_value, r0, m, l8, acc)
        return m8, l8, acc

    m = None
    for h in range(nheads + 1):
        h_score = h if h < nheads else None
        h_value = h - 1 if h > 0 else None
        carry = (jnp.full((8, tq), NEG_INF, F32), jnp.zeros((8, tq), F32), jnp.zeros((dv, tq), F32))
        carry = _for_tiles(
            nt, tk,
            lambda r0, c: tile_step(h_score, h_value, m, r0, c,
                                    None if (bias_tile is None or h_score is None) else bias_tile(r0)),
            carry)
        if diag_bias is not None:
            carry = tile_step(h_score, h_value, m, pl.multiple_of(nt * tk, tk), carry, diag_bias)
        m8, l8, acc = carry
        if h_value is not None:
            write_out(h_value, acc / jnp.sum(l8, axis=0, keepdims=True))
        m = jnp.max(m8, axis=0, keepdims=True)


def _transpose32(words):
    a = list(words)
    j, mask = 16, 0x0000FFFF
    while j:
        m = np.array(mask, np.uint32).view(np.int32)[()]
        for k in range(32):
            if not k & j:
                t = (lax.shift_right_logical(a[k], np.int32(j)) ^ a[k + j]) & m
                a[k + j] = a[k + j] ^ t
                a[k] = a[k] ^ lax.shift_left(t, np.int32(j))
        j >>= 1
        mask ^= (mask << j) & 0xFFFFFFFF
    return a


def _dsa_kernel(iqt_ref, iwt_ref, ik_ref, qt_ref, k_ref, vt_ref, ya_out,
                keys_ref, planes_ref, cand_ref, s0_ref, s1_ref, ot_ref, thr_ref, jcut_ref, trim_ref,
                *, tq, seq, ktop):
    tk = tq
    j = pl.program_id(1)
    nt = j + 1

    def put_keys(r0, diagonal):
        for sb in range(tk // IDX_ROWS):
            rs = pl.multiple_of(r0 + sb * IDX_ROWS, IDX_ROWS)
            q0 = (sb * IDX_ROWS // 128) * 128 if diagonal else 0
            ikt = ik_ref[0, pl.ds(rs, IDX_ROWS), :]
            acc = jnp.zeros((IDX_ROWS, tq - q0), F32)
            for hd in range(IDX_HEADS):
                sh = jnp.dot(ikt, iqt_ref[0, hd * IDX_DIM:(hd + 1) * IDX_DIM, q0:],
                             preferred_element_type=F32)
                acc = acc + jnp.maximum(sh, 0.0) * iwt_ref[0, hd:hd + 1, q0:]
            bits = lax.bitcast_convert_type(acc, I32)
            key = bits ^ ((bits >> 31) & np.int32(0x7FFFFFFF))
            if diagonal:
                kc = (sb * IDX_ROWS + lax.broadcasted_iota(I32, (IDX_ROWS, tq - q0), 0)) // CHUNK
                qc = (q0 + lax.broadcasted_iota(I32, (IDX_ROWS, tq - q0), 1)) // CHUNK
                key = jnp.where(kc <= qc, key, INT_MIN)
                if q0:
                    keys_ref[pl.ds(rs, IDX_ROWS), :q0] = jnp.full((IDX_ROWS, q0), INT_MIN, I32)
            keys_ref[pl.ds(rs, IDX_ROWS), q0:] = key

    def idx_tile(kt, c):
        put_keys(pl.multiple_of(kt * tk, tk), False)
        return c

    lax.fori_loop(0, j, idx_tile, 0)
    put_keys(pl.multiple_of(j * tk, tk), True)

    def count(pred):
        def body(r0, c8):
            kk = keys_ref[pl.ds(r0, tk), :]
            row = r0 + lax.broadcasted_iota(I32, (tk, tq), 0)
            return c8 + _col_reduce8(pred(kk, row).astype(I32), jnp.sum)
        c8 = _for_tiles(nt, tk, body, jnp.zeros((8, tq), I32))
        return jnp.sum(c8, axis=0, keepdims=True)

    keep_all = INT_MIN + np.int32(1)

    nw_all = seq // 32
    nw_tile = tk // 32
    if nw_all > nw_tile:
        @pl.when(j == 0)
        def _zero_tail():
            def zero_plane(b, c):
                planes_ref[b, nw_tile:, :] = jnp.zeros((nw_all - nw_tile, tq), I32)
                return c
            lax.fori_loop(0, 32, zero_plane, 0)

    @pl.when(nt * tk <= ktop)
    def _dense():
        thr_ref[...] = jnp.full((1, tq), keep_all, I32)
        trim_ref[0] = 0

    @pl.when(nt * tk > ktop)
    def _topk():
        def build_planes(g, c):
            rb = pl.multiple_of(g * PLANE_ROWS, PLANE_ROWS)
            wr = pl.multiple_of(g * 8, 8)
            for cs in range(tq // 128):
                lanes = slice(cs * 128, (cs + 1) * 128)
                words = [keys_ref[pl.ds(rb + 8 * i, 8), lanes] ^ INT_MIN for i in range(32)]
                for b, plane in enumerate(_transpose32(words)):
                    planes_ref[b, pl.ds(wr, 8), lanes] = plane
            return c

        lax.fori_loop(0, nt * (tk // PLANE_ROWS), build_planes, 0)

        word_row = lax.broadcasted_iota(I32, (nw_all, tq), 0)
        cand_ref[...] = jnp.where(word_row < nt * nw_tile, -1, 0).astype(I32)

        def ones_count(words):
            c8 = _col_reduce8(lax.population_count(words), jnp.sum)
            return jnp.sum(c8, axis=0, keepdims=True)

        def bit_step(i, carry):
            thr_u, left = carry
            b = 31 - i
            ones = ones_count(cand_ref[...] & planes_ref[b])
            take = ones >= left
            m = cand_ref[...]
            t = m & planes_ref[b]
            cand_ref[...] = jnp.where(take, t, m ^ t)
            return (jnp.where(take, thr_u | lax.shift_left(np.int32(1), b), thr_u),
                    jnp.where(take, left, left - ones))

        thr_u, need = lax.fori_loop(0, 32, bit_step,
                                    (jnp.zeros((1, tq), I32), jnp.full((1, tq), ktop, I32)))
        n_tied = ones_count(cand_ref[...])
        qpos = j * tq + lax.broadcasted_iota(I32, (1, tq), 1)
        all_kept = (qpos // CHUNK + 1) * CHUNK <= ktop
        thr = jnp.where(all_kept, keep_all, thr_u ^ INT_MIN)
        thr_ref[...] = thr
        trim_ref[0] = 0

        @pl.when(jnp.max(jnp.where(all_kept, 0, n_tied - need)) > 0)
        def _ties():
            nbits = int(seq - 1).bit_length()

            def row_step(i, jp):
                cand = jp | lax.shift_left(np.int32(1), nbits - 1 - i)
                c = count(lambda kk, row: (kk == thr) & (row < cand))
                return jnp.where(c < need, cand, jp)

            jcut_ref[...] = lax.fori_loop(0, nbits, row_step, jnp.zeros((1, tq), I32)) + 1
            trim_ref[0] = 1

    thr = thr_ref[...]

    def put_mask(r0, sel):
        keys_ref[pl.ds(r0, tk), :] = lax.bitcast_convert_type(
            jnp.where(sel, 0.0, NEG_INF).astype(F32), I32)

    @pl.when(trim_ref[0] == 0)
    def _mask_plain():
        def fill(r0, c):
            put_mask(r0, keys_ref[pl.ds(r0, tk), :] >= thr)
            return c
        _for_tiles(nt, tk, fill, 0)

    @pl.when(trim_ref[0] != 0)
    def _mask_trimmed():
        jcut = jcut_ref[...]

        def fill(r0, c):
            kk = keys_ref[pl.ds(r0, tk), :]
            row = r0 + lax.broadcasted_iota(I32, (tk, tq), 0)
            put_mask(r0, (kk > thr) | ((kk == thr) & (row < jcut)))
            return c
        _for_tiles(nt, tk, fill, 0)

    def write_out(hd, o):
        ot_ref[hd * A_HD:(hd + 1) * A_HD, :] = o

    _attention_heads(
        A_HEADS, nt, tk, tq, A_HD,
        lambda hd, r0: k_ref[0, pl.ds(r0, tk), :],
        lambda hd: qt_ref[0, hd * A_HD:(hd + 1) * A_HD, :],
        lambda hd, r0: vt_ref[0, :, pl.ds(r0, tk)],
        (s0_ref, s1_ref), write_out,
        bias_tile=lambda r0: lax.bitcast_convert_type(keys_ref[pl.ds(r0, tk), :], F32))
    ya_out[0] = ot_ref[...].T.astype(BF16)


def _dsa_call(iqt, iwt, ik, qt, k, vt, tq):
    b, _, s = qt.shape
    ktop = min(TOPK_MAX, s // 4)
    assert s % tq == 0 and tq % CHUNK == 0 and ktop % CHUNK == 0
    grid = (b, s // tq)
    qblk = lambda r: pl.BlockSpec((1, r, tq), lambda bi, j: (bi, 0, j))
    in_specs = [
        qblk(IDX_HEADS * IDX_DIM), qblk(IDX_HEADS),
        pl.BlockSpec((1, s, IDX_DIM), lambda bi, j: (bi, 0, 0)),
        qblk(A_WIDTH),
        pl.BlockSpec((1, s, A_HD), lambda bi, j: (bi, 0, 0)),
        pl.BlockSpec((1, A_HD, s), lambda bi, j: (bi, 0, 0)),
    ]
    return pl.pallas_call(
        functools.partial(_dsa_kernel, tq=tq, seq=s, ktop=ktop),
        grid=grid, in_specs=in_specs,
        out_specs=pl.BlockSpec((1, tq, A_WIDTH), lambda bi, j: (bi, j, 0)),
        out_shape=jax.ShapeDtypeStruct((b, s, A_WIDTH), BF16),
        scratch_shapes=[pltpu.VMEM((s, tq), I32), pltpu.VMEM((32, s // 32, tq), I32),
                        pltpu.VMEM((s // 32, tq), I32),
                        pltpu.VMEM((s, tq), F32), pltpu.VMEM((s, tq), F32),
                        pltpu.VMEM((A_WIDTH, tq), F32), pltpu.VMEM((1, tq), I32), pltpu.VMEM((1, tq), I32),
                        pltpu.SMEM((1,), I32)],
        compiler_params=_cparams(("parallel", "arbitrary")), name="dsa",
    )(iqt, iwt, ik, qt, k, vt)


def _mla_kernel(cqt_ref, ck_ref, cvt_ref, yc_out, s0_ref, s1_ref, ot_ref, *, tq):
    tk = tq
    j = pl.program_id(1)

    def write_out(hd, o):
        ot_ref[hd * C_VDIM:(hd + 1) * C_VDIM, :] = o

    _attention_heads(
        C_HEADS, j, tk, tq, C_VDIM,
        lambda hd, r0: ck_ref[0, hd, pl.ds(r0, tk), :],
        lambda hd: cqt_ref[0, hd * C_QK_PAD:(hd + 1) * C_QK_PAD, :],
        lambda hd, r0: cvt_ref[0, hd * C_VDIM:(hd + 1) * C_VDIM, pl.ds(r0, tk)],
        (s0_ref, s1_ref), write_out, diag_bias=_chunk_causal_bias(tk, tq))
    yc_out[0] = ot_ref[...].T.astype(BF16)


def _mla_call(cqt, ck, cvt, tq):
    b, _, s = cqt.shape
    assert s % tq == 0 and tq % CHUNK == 0
    grid = (b, s // tq)
    in_specs = [
        pl.BlockSpec((1, C_HEADS * C_QK_PAD, tq), lambda bi, j: (bi, 0, j)),
        pl.BlockSpec((1, C_HEADS, s, C_QK_PAD), lambda bi, j: (bi, 0, 0, 0)),
        pl.BlockSpec((1, C_WIDTH, s), lambda bi, j: (bi, 0, 0)),
    ]
    return pl.pallas_call(
        functools.partial(_mla_kernel, tq=tq),
        grid=grid, in_specs=in_specs,
        out_specs=pl.BlockSpec((1, tq, C_WIDTH), lambda bi, j: (bi, j, 0)),
        out_shape=jax.ShapeDtypeStruct((b, s, C_WIDTH), BF16),
        scratch_shapes=[pltpu.VMEM((s, tq), F32), pltpu.VMEM((s, tq), F32), pltpu.VMEM((C_WIDTH, tq), F32)],
        compiler_params=_cparams(("parallel", "arbitrary")), name="mla",
    )(cqt, ck, cvt)


def _s5_kernel(u_ref, bblk_ref, cre_ref, cim_ref, lam_ref, dskip_ref, wglu_ref, yb_out,
               state_ref, bb_ref, xs_ref, *, nb, tsteps):
    @pl.when(pl.program_id(0) == 0)
    def _init():
        state_ref[...] = jnp.zeros_like(state_ref)

    ar, ai, step = lam_ref[0:1, :], lam_ref[1:2, :], jnp.exp(lam_ref[2:3, :])
    mag = jnp.exp(ar * step)
    abar_re = mag * jnp.cos(ai * step)
    abar_im = mag * jnp.sin(ai * step)
    den = ar * ar + ai * ai
    nr = abar_re - 1.0
    f_re = (nr * ar + abar_im * ai) / den
    f_im = (abar_im * ar - nr * ai) / den

    b_re, b_im = bblk_ref[:, :S5_CH], bblk_ref[:, S5_CH:]
    u = u_ref[...]
    ub = u.astype(BF16)
    bb_ref[:, :S5_CH] = jnp.dot(ub, (f_re * b_re - f_im * b_im).astype(BF16), preferred_element_type=F32)
    bb_ref[:, S5_CH:] = jnp.dot(ub, (f_re * b_im + f_im * b_re).astype(BF16), preferred_element_type=F32)

    a_re = jnp.broadcast_to(abar_re, (nb, S5_CH))
    a_im = jnp.broadcast_to(abar_im, (nb, S5_CH))

    def scan_step(t, carry):
        xr, xi = carry
        r0 = pl.multiple_of(t * nb, nb)
        nxr = a_re * xr - a_im * xi + bb_ref[pl.ds(r0, nb), :S5_CH]
        nxi = a_re * xi + a_im * xr + bb_ref[pl.ds(r0, nb), S5_CH:]
        xs_ref[pl.ds(r0, nb), :S5_CH] = nxr
        xs_ref[pl.ds(r0, nb), S5_CH:] = nxi
        return nxr, nxi

    xr, xi = lax.fori_loop(0, tsteps, scan_step, (state_ref[:, :S5_CH], state_ref[:, S5_CH:]),
                           unroll=4)
    state_ref[:, :S5_CH] = xr
    state_ref[:, S5_CH:] = xi

    y = (jnp.dot(xs_ref[:, :S5_CH].astype(BF16), cre_ref[...], preferred_element_type=F32)
         - jnp.dot(xs_ref[:, S5_CH:].astype(BF16), cim_ref[...], preferred_element_type=F32)
         + dskip_ref[...] * u)
    g = 0.5 * y * (1.0 + jnp.tanh(np.float32(np.sqrt(2.0 / np.pi)) * (y + 0.044715 * (y * y * y))))
    gate = jax.nn.sigmoid(jnp.dot(g.astype(BF16), wglu_ref[...], preferred_element_type=F32))
    yb_out[...] = (g * gate).astype(BF16)


def _s5_call(u2d, bblk, cre, cim, lam, dskip, wglu, nb, tsteps):
    rows = u2d.shape[0]
    blk = nb * tsteps
    assert rows % blk == 0 and nb % 8 == 0
    full = lambda a: pl.BlockSpec(a.shape, lambda i: (0,) * a.ndim)
    return pl.pallas_call(
        functools.partial(_s5_kernel, nb=nb, tsteps=tsteps),
        grid=(rows // blk,),
        in_specs=[pl.BlockSpec((blk, S5_WIDTH), lambda i: (i, 0)),
                  full(bblk), full(cre), full(cim), full(lam), full(dskip), full(wglu)],
        out_specs=pl.BlockSpec((blk, S5_WIDTH), lambda i: (i, 0)),
        out_shape=jax.ShapeDtypeStruct((rows, S5_WIDTH), BF16),
        scratch_shapes=[pltpu.VMEM((nb, 2 * S5_CH), F32), pltpu.VMEM((blk, 2 * S5_CH), F32),
                        pltpu.VMEM((blk, 2 * S5_CH), F32)],
        compiler_params=_cparams(("arbitrary",)), name="s5",
    )(u2d, bblk, cre, cim, lam, dskip, wglu)


def _out_body(x_ref, g_ref, ya_ref, yb_ref, yc_ref, wo_ref):
    g = g_ref[0].astype(F32)
    sg = g * jax.nn.sigmoid(g)
    ma = (ya_ref[0].astype(F32) * sg[:, :A_WIDTH]).astype(BF16)
    mb = (yb_ref[...].astype(F32) * sg[:, A_WIDTH:A_WIDTH + S5_WIDTH]).astype(BF16)
    mc = (yc_ref[0].astype(F32) * sg[:, A_WIDTH + S5_WIDTH:]).astype(BF16)
    return (x_ref[0]
            + jnp.dot(ma, wo_ref[:A_WIDTH, :], preferred_element_type=F32)
            + jnp.dot(mb, wo_ref[A_WIDTH:A_WIDTH + S5_WIDTH, :], preferred_element_type=F32)
            + jnp.dot(mc, wo_ref[A_WIDTH + S5_WIDTH:, :], preferred_element_type=F32))


def _out_kernel(x_ref, g_ref, ya_ref, yb_ref, yc_ref, wo_ref, o_ref):
    o_ref[0] = _out_body(x_ref, g_ref, ya_ref, yb_ref, yc_ref, wo_ref)


def _out_prep_kernel(x_ref, g_ref, ya_ref, yb_ref, yc_ref, wo_ref, *refs):
    prep_in, o_ref, prep_out = refs[:7], refs[7], refs[8:]
    x = _out_body(x_ref, g_ref, ya_ref, yb_ref, yc_ref, wo_ref)
    o_ref[0] = x
    _prep_body(x, *prep_in, *prep_out)


def _out_specs(d, ts, wo):
    tok = lambda w: pl.BlockSpec((1, ts, w), lambda bi, i: (bi, i, 0))
    return [tok(d), tok(1024), tok(A_WIDTH),
            pl.BlockSpec((ts, S5_WIDTH), lambda bi, i: (i, bi)), tok(C_WIDTH),
            pl.BlockSpec(wo.shape, lambda bi, i: (0, 0))]


def _out_call(x, g, ya, yb, yc, wo, ts):
    b, s, d = x.shape
    return pl.pallas_call(
        _out_kernel, grid=(b, s // ts), in_specs=_out_specs(d, ts, wo),
        out_specs=pl.BlockSpec((1, ts, d), lambda bi, i: (bi, i, 0)),
        out_shape=jax.ShapeDtypeStruct((b, s, d), F32),
        compiler_params=_cparams(("parallel", "parallel")), name="outproj",
    )(x, g, ya, yb, yc, wo)


def _out_prep_call(x, g, ya, yb, yc, wo, ng, wn, wt, gcol, rope, wuq, wukv, ts):
    b, s, d = x.shape
    in_specs, out_shape, out_specs = _prep_specs(b, s, ts, ng, wn, wt, gcol, wuq, wukv)
    xspec = pl.BlockSpec((1, ts, d), lambda bi, i: (bi, i, 0))
    return pl.pallas_call(
        _out_prep_kernel, grid=(b, s // ts),
        in_specs=_out_specs(d, ts, wo) + in_specs,
        out_specs=(xspec,) + tuple(out_specs),
        out_shape=(jax.ShapeDtypeStruct((b, s, d), F32),) + tuple(out_shape),
        compiler_params=_cparams(("parallel", "parallel")), name="outprep",
    )(x, g, ya, yb, yc, wo, ng, wn, wt, gcol, rope, wuq, wukv)


def _rope_table(s):
    pos = jnp.arange(s, dtype=jnp.int32).astype(F32)

    def cs(d):
        half = d // 2
        inv = ROPE_THETA ** (-jnp.arange(half, dtype=F32) * 2.0 / d)
        ang = pos[:, None] * inv[None, :]
        return jnp.cos(ang).T, jnp.sin(ang).T

    c64, s64 = cs(A_HD)
    c32, s32 = cs(IDX_DIM)
    return jnp.concatenate([c64, s64, c32, s32], axis=0)


def _block_diag(w):
    g, r, c = w.shape
    eye = jnp.eye(g, dtype=w.dtype)
    return (w[:, :, None, :] * eye[:, None, :, None]).reshape(g * r, g * c)


def _layer_operands(l, norm_g, w_in, attn_q_norm, attn_k_norm, mla_q_lora_norm, mla_kv_lora_norm,
                    mla_w_uq, mla_w_ukv, mla_q_norm, mla_k_norm, ssm_a_re, ssm_a_im, ssm_b_re,
                    ssm_b_im, ssm_c_re, ssm_c_im, ssm_d, ssm_log_step, ssm_w_glu, w_out):
    pts = [int(v) for v in np.cumsum(SPLIT_SIZES)[:-1]]
    (qa, ka, va, iq, ik, iw, ga, u, gb, cq, ckv, kpe, gc) = jnp.split(w_in[l], pts, axis=-1)
    wn = jnp.concatenate([ga, gb, gc, u], axis=1).astype(BF16)
    pad = jnp.zeros((w_in.shape[1], T_END - T_IW - IDX_HEADS), w_in.dtype)
    wt = jnp.concatenate([qa, ka, va, iq, ik, cq, ckv, kpe, iw, pad], axis=1).T.astype(BF16)
    gcol = jnp.concatenate([attn_q_norm[l], attn_k_norm[l], mla_q_lora_norm[l], mla_kv_lora_norm[l],
                            mla_q_norm[l], mla_k_norm[l]]).astype(F32)[:, None]
    wuq = mla_w_uq[l].T.reshape(C_HEADS, C_QK, Q_LORA)
    wuq = jnp.pad(wuq, ((0, 0), (0, C_QK_PAD - C_QK), (0, 0))).reshape(C_HEADS * C_QK_PAD, Q_LORA).astype(BF16)
    wukv = mla_w_ukv[l].T.astype(BF16)
    bblk = jnp.concatenate([_block_diag(jnp.swapaxes(ssm_b_re[l], 1, 2)),
                            _block_diag(jnp.swapaxes(ssm_b_im[l], 1, 2))], axis=1).astype(F32)
    cre = _block_diag(jnp.swapaxes(ssm_c_re[l], 1, 2)).astype(BF16)
    cim = _block_diag(jnp.swapaxes(ssm_c_im[l], 1, 2)).astype(BF16)
    lam = jnp.stack([ssm_a_re[l].reshape(-1), ssm_a_im[l].reshape(-1),
                     jnp.repeat(ssm_log_step[l], S5_STATE)]).astype(F32)
    lam = jnp.concatenate([lam, jnp.zeros((5, S5_CH), F32)], axis=0)
    dskip = ssm_d[l].reshape(1, S5_WIDTH).astype(F32)
    return dict(ng=norm_g[l][None, :].astype(F32), wn=wn, wt=wt, gcol=gcol, wuq=wuq, wukv=wukv,
                bblk=bblk, cre=cre, cim=cim, lam=lam, dskip=dskip, wglu=ssm_w_glu[l].astype(BF16),
                wo=w_out[l].astype(BF16))


def kernel(x, norm_g, w_in, attn_q_norm, attn_k_norm, mla_q_lora_norm, mla_kv_lora_norm, mla_w_uq, mla_w_ukv, mla_q_norm, mla_k_norm, ssm_a_re, ssm_a_im, ssm_b_re, ssm_b_im, ssm_c_re, ssm_c_im, ssm_d, ssm_log_step, ssm_w_glu, w_out):
    b, s, _ = x.shape
    ts = min(512, s)
    tq = min(512, s)
    s5_steps = 128
    rope = _rope_table(s)
    params = (norm_g, w_in, attn_q_norm, attn_k_norm, mla_q_lora_norm, mla_kv_lora_norm, mla_w_uq,
              mla_w_ukv, mla_q_norm, mla_k_norm, ssm_a_re, ssm_a_im, ssm_b_re, ssm_b_im, ssm_c_re,
              ssm_c_im, ssm_d, ssm_log_step, ssm_w_glu, w_out)
    depth = w_in.shape[0]
    layers = [_layer_operands(l, *params) for l in range(depth)]
    prep_w = lambda p: (p["ng"], p["wn"], p["wt"], p["gcol"], rope, p["wuq"], p["wukv"])
    prepped = _prep_call(x, *prep_w(layers[0]), ts)
    for l, p in enumerate(layers):
        (g, u, qt, k, vt, iqt, ik, iwt, cqt, ck, cvt) = prepped
        ya = _dsa_call(iqt, iwt, ik, qt, k, vt, tq)
        yc = _mla_call(cqt, ck, cvt, tq)
        yb = _s5_call(u.reshape(s * b, S5_WIDTH), p["bblk"], p["cre"], p["cim"], p["lam"],
                      p["dskip"], p["wglu"], b, s5_steps).reshape(s, b * S5_WIDTH)
        if l + 1 < depth:
            x, *prepped = _out_prep_call(x, g, ya, yb, yc, p["wo"], *prep_w(layers[l + 1]), ts)
        else:
            x = _out_call(x, g, ya, yb, yc, p["wo"], ts)
    return x
```
